```python
import math
import jax, jax.numpy as jnp
from jax import lax
import numpy as np

D_MODEL = 2048
BATCH = 4
SEQ = 2048
DEPTH = 4
DEC_BATCH = 8
DEC_SEQ = 4
PAST_LEN = 16384
PAGE_SIZE = 128

MIX_WIDTH = D_MODEL
HEAD_DIM = 128
DN_WIDTH = MIX_WIDTH // 2
MOBA_WIDTH = MIX_WIDTH - DN_WIDTH
DN_HEADS = DN_WIDTH // HEAD_DIM
MOBA_HEADS = MOBA_WIDTH // HEAD_DIM
DN_CONV_CH = 3 * DN_WIDTH
CONV_W = 4
DN_CHUNK = 64
MOBA_BLOCK = 256
MOBA_TOPK = 3
Q_CHUNK = 16
REL_BUCKETS = 32
REL_MAX_DIST = 128
FFN_HIDDEN = -(-(8 * D_MODEL) // (3 * 256)) * 256
RMS_EPS = 1e-6
L2_EPS = 1e-6
Z_OFF = DN_CONV_CH
A_OFF = Z_OFF + DN_WIDTH
B_OFF = A_OFF + DN_HEADS
MB_OFF = B_OFF + DN_HEADS
IN_COLS = MB_OFF + 3 * MOBA_WIDTH

kernel_name = 'hymba_style_gdn_moba_decoder_step'


def rmsnorm(x, w):
    xf = x.astype(jnp.float32)
    y = xf * lax.rsqrt(jnp.mean(xf * xf, axis=-1, keepdims=True) + RMS_EPS)
    return (y * w.astype(jnp.float32)).astype(x.dtype)


def l2norm(x):
    xf = x.astype(jnp.float32)
    return xf * lax.rsqrt(jnp.sum(xf * xf, axis=-1, keepdims=True) + L2_EPS)


def causal_conv(x, prev, w):
    T = x.shape[1]
    xc = jnp.concatenate([prev.astype(x.dtype), x], axis=1)
    y = xc[:, 0:T] * w[0]
    for i in range(1, CONV_W):
        y = y + xc[:, i:i + T] * w[i]
    return jax.nn.silu(y), xc[:, -(CONV_W - 1):]


def gated_delta_rule(q, k, v, g, beta, S0):
    f32 = jnp.float32
    B, T, H, DK = q.shape
    DV = v.shape[-1]
    C = min(DN_CHUNK, T)
    N = -(-T // C)
    pad = N * C - T

    def prep(a):
        a = a.astype(f32)
        a = jnp.pad(a, [(0, 0), (0, pad)] + [(0, 0)] * (a.ndim - 2))
        a = jnp.moveaxis(a, 2, 1)
        return a.reshape(a.shape[:2] + (N, C) + a.shape[3:])

    q, k, v, g, beta = prep(q), prep(k), prep(v), prep(g), prep(beta)
    q = q * (DK ** -0.5)
    gc = jnp.cumsum(g, axis=-1)
    idx = jnp.arange(C)
    tril = idx[:, None] >= idx[None, :]
    strict = idx[:, None] > idx[None, :]
    decay = jnp.exp(jnp.where(tril, gc[..., :, None] - gc[..., None, :], -jnp.inf))
    k_beta = k * beta[..., None]
    v_beta = v * beta[..., None]
    Lm = jnp.where(strict, jnp.einsum('bhncd,bhnjd->bhncj', k_beta, k) * decay, 0.0)
    A = Lm + jnp.eye(C, dtype=f32)
    u = lax.linalg.triangular_solve(A, v_beta, left_side=True, lower=True, unit_diagonal=True)
    w = lax.linalg.triangular_solve(A, k_beta * jnp.exp(gc)[..., None], left_side=True, lower=True,
                                    unit_diagonal=True)
    attn = jnp.einsum('bhncd,bhnjd->bhncj', q, k) * decay
    q_dec = q * jnp.exp(gc)[..., None]
    g_last = gc[..., -1]
    k_dec = k * jnp.exp(g_last[..., None] - gc)[..., None]

    def step(S, xs):
        attn_i, u_i, w_i, qd_i, kd_i, gl_i = xs
        v_new = u_i - jnp.einsum('bhcd,bhde->bhce', w_i, S)
        o = jnp.einsum('bhcd,bhde->bhce', qd_i, S) + jnp.einsum('bhcj,bhje->bhce', attn_i, v_new)
        S = S * jnp.exp(gl_i)[..., None, None] + jnp.einsum('bhcd,bhce->bhde', kd_i, v_new)
        return S, o

    xs = tuple(jnp.moveaxis(a, 2, 0) for a in (attn, u, w, q_dec, k_dec, g_last))
    S, o = lax.scan(step, S0.astype(f32), xs)
    o = jnp.moveaxis(o, 0, 2).reshape(B, H, N * C, DV)[:, :, :T]
    return jnp.moveaxis(o, 1, 2), S


def rel_bucket(dist):
    n = jnp.maximum(dist, 0)
    max_exact = REL_BUCKETS // 2
    nf = jnp.maximum(n, 1).astype(jnp.float32)
    large = max_exact + (jnp.log(nf / max_exact) / math.log(REL_MAX_DIST / max_exact)
                         * (REL_BUCKETS - max_exact)).astype(jnp.int32)
    large = jnp.minimum(large, REL_BUCKETS - 1)
    return jnp.where(n < max_exact, n, large)


def moba_attention(q, k, v, q0, rel_bias):
    f32 = jnp.float32
    B, T, H, D = q.shape
    L = k.shape[1]
    nb = -(-L // MOBA_BLOCK)
    pad = nb * MOBA_BLOCK - L

    def to_blocks(a):
        a = jnp.pad(a, ((0, 0), (0, pad), (0, 0), (0, 0)))
        return a.reshape(B, nb, MOBA_BLOCK, H, D).transpose(0, 3, 1, 2, 4)

    kb, vb = to_blocks(k), to_blocks(v)
    k_mean = jnp.mean(kb.astype(f32), axis=3)
    qf = jnp.transpose(q, (0, 2, 1, 3)).astype(f32)
    pos = q0 + jnp.arange(T)
    own = pos // MOBA_BLOCK
    gate = jnp.einsum('bhtd,bhnd->bhtn', qf, k_mean)
    gate = jnp.where(jnp.arange(nb)[None, :] < own[:, None], gate, -jnp.inf)
    if nb < MOBA_TOPK:
        gate = jnp.pad(gate, ((0, 0), (0, 0), (0, 0), (0, MOBA_TOPK - nb)), constant_values=-jnp.inf)
    _, sel = lax.top_k(gate, MOBA_TOPK)
    sel_ok = sel < own[None, None, :, None]
    sel = jnp.minimum(sel, nb - 1)

    qb = math.gcd(T, Q_CHUNK)
    nc = T // qb

    def split(a):
        a = a.reshape(a.shape[:2] + (nc, qb) + a.shape[3:])
        return jnp.moveaxis(a, 2, 0)

    b_idx = jnp.arange(B)[:, None, None, None]
    h_idx = jnp.arange(H)[None, :, None, None]
    table = rel_bias.astype(f32).T
    offs = jnp.arange(MOBA_BLOCK)
    scale = D ** -0.5
    n_sel = MOBA_TOPK * MOBA_BLOCK

    def one_chunk(args):
        qc, sc, okc, pc = args
        oc = pc[0] // MOBA_BLOCK
        k_sel = kb[b_idx, h_idx, sc].astype(f32)
        v_sel = vb[b_idx, h_idx, sc].astype(f32)
        k_own = lax.dynamic_index_in_dim(kb, oc, axis=2, keepdims=False).astype(f32)
        v_own = lax.dynamic_index_in_dim(vb, oc, axis=2, keepdims=False).astype(f32)
        d_sel = pc[None, None, :, None, None] - (sc[..., None] * MOBA_BLOCK + offs)
        d_own = pc[:, None] - (oc * MOBA_BLOCK + offs)[None, :]
        bias_sel = table[h_idx[..., None], rel_bucket(d_sel)]
        bias_own = table[:, rel_bucket(d_own)]
        l_sel = jnp.einsum('bhqd,bhqskd->bhqsk', qc, k_sel) * scale + bias_sel
        l_sel = jnp.where(okc[..., None], l_sel, -jnp.inf)
        l_own = jnp.einsum('bhqd,bhkd->bhqk', qc, k_own) * scale + bias_own[None]
        l_own = jnp.where((d_own >= 0)[None, None], l_own, -jnp.inf)
        p = jax.nn.softmax(jnp.concatenate([l_sel.reshape(B, H, qb, n_sel), l_own], axis=-1), axis=-1)
        p_sel = p[..., :n_sel].reshape(B, H, qb, MOBA_TOPK, MOBA_BLOCK)
        return (jnp.einsum('bhqsk,bhqskd->bhqd', p_sel, v_sel)
                + jnp.einsum('bhqk,bhkd->bhqd', p[..., n_sel:], v_own))

    out = lax.map(one_chunk, (split(qf), split(sel), split(sel_ok), pos.reshape(nc, qb)))
    out = jnp.moveaxis(out, 0, 2).reshape(B, H, T, D)
    return jnp.transpose(out, (0, 2, 1, 3))


def trunk_layer(x, conv_prev, delta_prev, k_past, v_past, q0, w_in, conv_w, a_log, dt_bias, gn_w,
                w_out, norm_mix, norm_ffn, w_gate, w_up, w_down, rel_bias):
    f32 = jnp.float32
    B, T, _ = x.shape
    h = rmsnorm(x, norm_mix)
    proj = h @ w_in
    qkv_dn = proj[..., :Z_OFF]
    z = proj[..., Z_OFF:A_OFF]
    a = proj[..., A_OFF:B_OFF]
    b = proj[..., B_OFF:MB_OFF]
    qkv_mb = proj[..., MB_OFF:]
    conv_out, conv_new = causal_conv(qkv_dn, conv_prev, conv_w)
    q_dn, k_dn, v_dn = [c.reshape(B, T, DN_HEADS, HEAD_DIM) for c in jnp.split(conv_out, 3, axis=-1)]
    beta = jax.nn.sigmoid(b.astype(f32))
    g = -jnp.exp(a_log.astype(f32)) * jax.nn.softplus(a.astype(f32) + dt_bias.astype(f32))
    o_dn, delta_new = gated_delta_rule(l2norm(q_dn), l2norm(k_dn), v_dn, g, beta, delta_prev)
    o_dn = rmsnorm(o_dn, gn_w) * jax.nn.silu(z.astype(f32)).reshape(B, T, DN_HEADS, HEAD_DIM)
    q_mb, k_mb, v_mb = [c.reshape(B, T, MOBA_HEADS, HEAD_DIM) for c in jnp.split(qkv_mb, 3, axis=-1)]
    if k_past is None:
        k_all, v_all = k_mb, v_mb
    else:
        k_all = jnp.concatenate([k_past.astype(k_mb.dtype), k_mb], axis=1)
        v_all = jnp.concatenate([v_past.astype(v_mb.dtype), v_mb], axis=1)
    o_mb = moba_attention(q_mb, k_all, v_all, q0, rel_bias)
    mix = jnp.concatenate([o_dn.reshape(B, T, DN_WIDTH), o_mb.reshape(B, T, MOBA_WIDTH)],
                          axis=-1).astype(x.dtype)
    x = x + mix @ w_out
    h = rmsnorm(x, norm_ffn)
    x = x + (jax.nn.silu(h @ w_gate) * (h @ w_up)) @ w_down
    return x, conv_new, delta_new, k_mb, v_mb


def setup_inputs(seed: int = 0) -> dict:
    key = jax.random.key(seed)
    ks = jax.random.split(key, 20)
    f32 = jnp.float32
    n_pages = PAST_LEN // PAGE_SIZE
    n_used = DEC_BATCH * n_pages
    n_pool = n_used + n_used // 4

    def nrm(k, shape, scale):
        return jax.random.normal(k, shape, f32) * scale

    x_prompt = nrm(ks[0], (BATCH, SEQ, D_MODEL), 1.0)
    x_sample = nrm(ks[1], (DEC_BATCH, DEC_SEQ, D_MODEL), 1.0)
    cache_k = nrm(ks[2], (DEPTH, n_pool, PAGE_SIZE, MOBA_HEADS, HEAD_DIM), 1.0)
    cache_v = nrm(ks[3], (DEPTH, n_pool, PAGE_SIZE, MOBA_HEADS, HEAD_DIM), 1.0)
    page_table = jax.random.permutation(ks[4], n_pool)[:n_used].reshape(DEC_BATCH, n_pages).astype(jnp.int32)
    state_conv = nrm(ks[5], (DEPTH, DEC_BATCH, CONV_W - 1, DN_CONV_CH), 1.0)
    state_delta = nrm(ks[6], (DEPTH, DEC_BATCH, DN_HEADS, HEAD_DIM, HEAD_DIM), HEAD_DIM ** -0.5)
    w_in = nrm(ks[7], (DEPTH, D_MODEL, IN_COLS), D_MODEL ** -0.5)
    conv_w = nrm(ks[8], (DEPTH, CONV_W, DN_CONV_CH), CONV_W ** -0.5)
    a_log = jnp.log(jax.random.uniform(ks[9], (DEPTH, DN_HEADS), f32, 1.0, 16.0))
    dt = jnp.exp(jax.random.uniform(ks[10], (DEPTH, DN_HEADS), f32, math.log(1e-3), math.log(1e-1)))
    dt_bias = dt + jnp.log(-jnp.expm1(-dt))
    gn_w = 1.0 + nrm(ks[11], (DEPTH, HEAD_DIM), 0.02)
    w_out = nrm(ks[12], (DEPTH, MIX_WIDTH, D_MODEL), MIX_WIDTH ** -0.5)
    norm_mix = 1.0 + nrm(ks[13], (DEPTH, D_MODEL), 0.02)
    norm_ffn = 1.0 + nrm(ks[14], (DEPTH, D_MODEL), 0.02)
    w_gate = nrm(ks[15], (DEPTH, D_MODEL, FFN_HIDDEN), D_MODEL ** -0.5)
    w_up = nrm(ks[16], (DEPTH, D_MODEL, FFN_HIDDEN), D_MODEL ** -0.5)
    w_down = nrm(ks[17], (DEPTH, FFN_HIDDEN, D_MODEL), FFN_HIDDEN ** -0.5)
    rel_bias = nrm(ks[18], (REL_BUCKETS, MOBA_HEADS), 0.5)
    final_norm = 1.0 + nrm(ks[19], (D_MODEL,), 0.02)
    return {'x_prompt': x_prompt, 'x_sample': x_sample, 'cache_k': cache_k, 'cache_v': cache_v,
            'page_table': page_table, 'state_conv': state_conv, 'state_delta': state_delta,
            'w_in': w_in, 'conv_w': conv_w, 'a_log': a_log, 'dt_bias': dt_bias, 'gn_w': gn_w,
            'w_out': w_out, 'norm_mix': norm_mix, 'norm_ffn': norm_ffn, 'w_gate': w_gate,
            'w_up': w_up, 'w_down': w_down, 'rel_bias': rel_bias, 'final_norm': final_norm}


def reference(x_prompt, x_sample, cache_k, cache_v, page_table, state_conv, state_delta, w_in, conv_w,
              a_log, dt_bias, gn_w, w_out, norm_mix, norm_ffn, w_gate, w_up, w_down, rel_bias, final_norm):
    bp = x_prompt.shape[0]
    bs = x_sample.shape[0]
    past_len = page_table.shape[1] * cache_k.shape[2]
    conv0 = jnp.zeros((bp, CONV_W - 1, DN_CONV_CH), x_prompt.dtype)
    delta0 = jnp.zeros((bp, DN_HEADS, HEAD_DIM, HEAD_DIM), jnp.float32)
    yp, ys = x_prompt, x_sample
    kp, vp, cp, sp, kd, vd, cd, sd = [], [], [], [], [], [], [], []
    for l in range(DEPTH):
        wl = (w_in[l], conv_w[l], a_log[l], dt_bias[l], gn_w[l], w_out[l], norm_mix[l], norm_ffn[l],
              w_gate[l], w_up[l], w_down[l], rel_bias)
        yp, c_new, s_new, k_new, v_new = trunk_layer(yp, conv0, delta0, None, None, 0, *wl)
        kp.append(k_new); vp.append(v_new); cp.append(c_new); sp.append(s_new)
        k_past = cache_k[l, page_table].reshape(bs, past_len, MOBA_HEADS, HEAD_DIM)
        v_past = cache_v[l, page_table].reshape(bs, past_len, MOBA_HEADS, HEAD_DIM)
        ys, c_new, s_new, k_new, v_new = trunk_layer(ys, state_conv[l], state_delta[l], k_past, v_past,
                                                     past_len, *wl)
        kd.append(k_new); vd.append(v_new); cd.append(c_new); sd.append(s_new)
    y_prompt = rmsnorm(yp, final_norm)
    y_sample = rmsnorm(ys, final_norm)
    return (y_prompt, y_sample, jnp.stack(kp), jnp.stack(vp), jnp.stack(cp), jnp.stack(sp),
            jnp.stack(kd), jnp.stack(vd), jnp.stack(cd), jnp.stack(sd))
```

```python
import functools
import math

import jax
import jax.numpy as jnp
import numpy as np
from jax import lax
from jax.experimental import pallas as pl
from jax.experimental.pallas import tpu as pltpu

F32 = jnp.float32
BF16 = jnp.bfloat16
HI = lax.Precision.HIGHEST

HEAD_DIM = 128
N_HEADS = 8
GROUP_W = N_HEADS * HEAD_DIM
CONV_W = 4
DN_CHUNK = 64
MOBA_BLOCK = 256
MOBA_TOPK = 3
REL_BUCKETS = 32
REL_MAX_DIST = 128
RMS_EPS = 1e-6
L2_EPS = 1e-6
SUBLANES = 8
SAMPLE_ROWS = 8

QDN_BLK, KDN_BLK, VDN_BLK, Z_BLK, QMB_BLK, KMB_BLK, VMB_BLK = 0, 8, 16, 24, 32, 40, 48
MAIN_COLS = 56 * HEAD_DIM

VMEM_LIMIT = 56 * 1024 * 1024


def _bucket_thresholds():
    d = np.arange(0, 4 * REL_MAX_DIST)
    max_exact = REL_BUCKETS // 2
    nf = np.maximum(d, 1).astype(np.float32)
    large = max_exact + (np.log(nf / np.float32(max_exact)) / np.float32(math.log(REL_MAX_DIST / max_exact))
                         * (REL_BUCKETS - max_exact)).astype(np.int32)
    bucket = np.where(d < max_exact, d, np.minimum(large, REL_BUCKETS - 1))
    return tuple(int(np.argmax(bucket >= b)) for b in range(REL_BUCKETS))


BUCKET_THR = _bucket_thresholds()


def _cparams(*sem):
    return pltpu.CompilerParams(dimension_semantics=sem, vmem_limit_bytes=VMEM_LIMIT)


def _rms_rows(x, w_row):
    return x * lax.rsqrt(jnp.mean(x * x, axis=-1, keepdims=True) + RMS_EPS) * w_row


def _dot(a, b, precision=None):
    return jnp.dot(a, b, preferred_element_type=F32, precision=precision)


def _dot_nt(a, b, precision=None):
    return lax.dot_general(a, b, (((1,), (1,)), ((), ())), preferred_element_type=F32, precision=precision)


def _dot_tn(a, b, precision=None):
    return lax.dot_general(a, b, (((0,), (0,)), ((), ())), preferred_element_type=F32, precision=precision)


def _inproj_kernel(x_ref, nw_ref, w_ref, wab_ref, o_ref, oab_ref, h_scr):
    @pl.when(pl.program_id(1) == 0)
    def _():
        hb = _rms_rows(x_ref[...], nw_ref[...]).astype(BF16)
        h_scr[...] = hb
        oab_ref[...] = _dot(hb, wab_ref[...])

    o_ref[...] = _dot(h_scr[...], w_ref[...])


def _inproj(x, nw, w_main, w_ab, tm, tn=1024):
    m, d = x.shape
    n = w_main.shape[1]
    return pl.pallas_call(
        _inproj_kernel,
        grid=(m // tm, n // tn),
        in_specs=[pl.BlockSpec((tm, d), lambda i, j: (i, 0)),
                  pl.BlockSpec((1, d), lambda i, j: (0, 0)),
                  pl.BlockSpec((d, tn), lambda i, j: (0, j)),
                  pl.BlockSpec((d, HEAD_DIM), lambda i, j: (0, 0))],
        out_specs=[pl.BlockSpec((tm, tn), lambda i, j: (i, j)),
                   pl.BlockSpec((tm, HEAD_DIM), lambda i, j: (i, 0))],
        out_shape=[jax.ShapeDtypeStruct((m, n), F32), jax.ShapeDtypeStruct((m, HEAD_DIM), F32)],
        scratch_shapes=[pltpu.VMEM((tm, d), BF16)],
        compiler_params=_cparams("parallel", "arbitrary"),
        name="inproj",
    )(x, nw, w_main, w_ab)


def _outproj_kernel(x_ref, a_ref, b_ref, wa_ref, wb_ref, o_ref):
    o_ref[...] = x_ref[...] + (_dot(a_ref[...], wa_ref[...]) + _dot(b_ref[...], wb_ref[...]))


def _outproj(x, mix_dn, mix_mb, w_dn, w_mb, tm, tn=1024):
    m, d = x.shape
    k = mix_dn.shape[1]
    return pl.pallas_call(
        _outproj_kernel,
        grid=(m // tm, d // tn),
        in_specs=[pl.BlockSpec((tm, tn), lambda i, j: (i, j)),
                  pl.BlockSpec((tm, k), lambda i, j: (i, 0)),
                  pl.BlockSpec((tm, k), lambda i, j: (i, 0)),
                  pl.BlockSpec((k, tn), lambda i, j: (0, j)),
                  pl.BlockSpec((k, tn), lambda i, j: (0, j))],
        out_specs=pl.BlockSpec((tm, tn), lambda i, j: (i, j)),
        out_shape=jax.ShapeDtypeStruct((m, d), F32),
        compiler_params=_cparams("parallel", "arbitrary"),
        name="outproj",
    )(x, mix_dn, mix_mb, w_dn, w_mb)


def _ffn_up_kernel(x_ref, nw_ref, wg_ref, wu_ref, o_ref, h_scr):
    @pl.when(pl.program_id(1) == 0)
    def _():
        h_scr[...] = _rms_rows(x_ref[...], nw_ref[...]).astype(BF16)

    h = h_scr[...]
    g = _dot(h, wg_ref[...])
    u = _dot(h, wu_ref[...])
    o_ref[...] = (g * jax.nn.sigmoid(g) * u).astype(BF16)


def _ffn_up(x, nw, wg, wu, tm, tn=512):
    m, d = x.shape
    n = wg.shape[1]
    return pl.pallas_call(
        _ffn_up_kernel,
        grid=(m // tm, n // tn),
        in_specs=[pl.BlockSpec((tm, d), lambda i, j: (i, 0)),
                  pl.BlockSpec((1, d), lambda i, j: (0, 0)),
                  pl.BlockSpec((d, tn), lambda i, j: (0, j)),
                  pl.BlockSpec((d, tn), lambda i, j: (0, j))],
        out_specs=pl.BlockSpec((tm, tn), lambda i, j: (i, j)),
        out_shape=jax.ShapeDtypeStruct((m, n), BF16),
        scratch_shapes=[pltpu.VMEM((tm, d), BF16)],
        compiler_params=_cparams("parallel", "arbitrary"),
        name="ffn_up",
    )(x, nw, wg, wu)


def _ffn_down_kernel(x_ref, a_ref, w_ref, o_ref):
    o_ref[...] = x_ref[...] + _dot(a_ref[...], w_ref[...])


def _ffn_down(x, act, wd, tm, tn=512):
    m, d = x.shape
    k = act.shape[1]
    return pl.pallas_call(
        _ffn_down_kernel,
        grid=(m // tm, d // tn),
        in_specs=[pl.BlockSpec((tm, tn), lambda i, j: (i, j)),
                  pl.BlockSpec((tm, k), lambda i, j: (i, 0)),
                  pl.BlockSpec((k, tn), lambda i, j: (0, j))],
        out_specs=pl.BlockSpec((tm, tn), lambda i, j: (i, j)),
        out_shape=jax.ShapeDtypeStruct((m, d), F32),
        compiler_params=_cparams("parallel", "arbitrary"),
        name="ffn_down",
    )(x, act, wd)


def _final_norm_kernel(x_ref, nw_ref, o_ref):
    o_ref[...] = _rms_rows(x_ref[...], nw_ref[...])


def _final_norm(x, nw, tm):
    m, d = x.shape
    return pl.pallas_call(
        _final_norm_kernel,
        grid=(m // tm,),
        in_specs=[pl.BlockSpec((tm, d), lambda i: (i, 0)), pl.BlockSpec((1, d), lambda i: (0, 0))],
        out_specs=pl.BlockSpec((tm, d), lambda i: (i, 0)),
        out_shape=jax.ShapeDtypeStruct((m, d), F32),
        compiler_params=_cparams("parallel"),
        name="final_norm",
    )(x, nw)


def _deltanet_kernel(q_ref, k_ref, v_ref, z_ref, ab_ref, pq_ref, pk_ref, pv_ref, cq_ref, ck_ref, cv_ref,
                     alog_ref, dtb_ref, gn_ref, s0_ref, o_ref, s_ref, *, t_rows, t_valid):
    c = DN_CHUNK
    h = pl.program_id(1)
    n_chunks = max(t_rows // c, 1)
    rows = min(t_rows, c)

    ii = lax.broadcasted_iota(jnp.int32, (c, c), 0)
    jj = lax.broadcasted_iota(jnp.int32, (c, c), 1)
    tril = ii >= jj
    strict = ii > jj
    eye = ii == jj
    tril_f = tril.astype(F32)
    ones_cc = jnp.ones((c, c), F32)
    eye_f = eye.astype(F32)
    lane = lax.broadcasted_iota(jnp.int32, (c, HEAD_DIM), 1)
    row = lax.broadcasted_iota(jnp.int32, (c, HEAD_DIM), 0)
    neg_decay_rate = -jnp.exp(alog_ref[...])
    dtb = dtb_ref[...]
    gn = gn_ref[...]
    cq, ck, cv = cq_ref[...], ck_ref[...], cv_ref[...]

    s_ref[...] = s0_ref[...]

    def load(ref, r0):
        x = ref[pl.ds(r0, rows), :]
        if rows < c:
            x = jnp.concatenate([x, jnp.zeros((c - rows, HEAD_DIM), F32)], axis=0)
        return x

    def conv_silu(tail, x, taps):
        win = jnp.concatenate([tail, x], axis=0)
        y = win[5:5 + c] * taps[0:1]
        for i in range(1, CONV_W):
            y = y + win[5 + i:5 + i + c] * taps[i:i + 1]
        return y * jax.nn.sigmoid(y)

    def l2n(x):
        return x * lax.rsqrt(jnp.sum(x * x, axis=-1, keepdims=True) + L2_EPS)

    def chunk(ci, tails):
        tq, tk, tv = tails
        r0 = pl.multiple_of(ci * rows, rows)
        xq, xk, xv = load(q_ref, r0), load(k_ref, r0), load(v_ref, r0)
        q = l2n(conv_silu(tq, xq, cq)) * (HEAD_DIM ** -0.5)
        k = l2n(conv_silu(tk, xk, ck))
        v = conv_silu(tv, xv, cv)

        ab = load(ab_ref, r0)
        valid = (row + ci * c) < t_valid
        z_ab = ab + dtb
        softplus = jnp.maximum(z_ab, 0.0) + jnp.log1p(jnp.exp(-jnp.abs(z_ab)))
        g_all = jnp.where(valid & (lane == h), neg_decay_rate * softplus, 0.0)
        beta_all = jnp.where(valid & (lane == N_HEADS + h), jax.nn.sigmoid(ab), 0.0)
        beta = jnp.sum(beta_all, axis=1, keepdims=True)
        gc = jnp.sum(_dot(tril_f, g_all, HI), axis=1, keepdims=True)
        gc_row = _dot(ones_cc, eye_f * gc, HI)
        decay = jnp.where(tril, jnp.exp(jnp.where(tril, gc - gc_row, 0.0)), 0.0)
        e_gc = jnp.exp(gc)
        g_last = gc[c - 1:c, :]

        kb = k * beta
        vb = v * beta
        neg_l = jnp.where(strict, -(_dot_nt(kb, k, HI) * decay), 0.0)
        tinv = eye_f + neg_l
        p = neg_l
        for _ in range(5):
            p = _dot(p, p, HI)
            tinv = tinv + _dot(tinv, p, HI)
        u = _dot(tinv, vb, HI)
        w = _dot(tinv, kb * e_gc, HI)
        attn = jnp.where(tril, _dot_nt(q, k, HI) * decay, 0.0)
        s = s_ref[...]
        v_new = u - _dot(w, s, HI)
        o = _dot(q * e_gc, s, HI) + _dot(attn, v_new, HI)
        s_ref[...] = s * jnp.exp(g_last) + _dot_tn(k * jnp.exp(g_last - gc), v_new, HI)

        on = o * lax.rsqrt(jnp.mean(o * o, axis=-1, keepdims=True) + RMS_EPS) * gn
        zc = load(z_ref, r0)
        out = (on * (zc * jax.nn.sigmoid(zc))).astype(o_ref.dtype)
        o_ref[pl.ds(r0, rows), :] = out[:rows]
        return (xq[c - SUBLANES:], xk[c - SUBLANES:], xv[c - SUBLANES:])

    lax.fori_loop(0, n_chunks, chunk, (pq_ref[...], pk_ref[...], pv_ref[...]))


def _deltanet(proj, proj_ab, conv_prev, conv_w, alog_row, dtb_row, gn_row, s0, layer, n_seq, t_rows, t_valid):
    lp = layer if conv_prev.shape[0] > 1 else 0

    def col(off):
        return pl.BlockSpec((t_rows, HEAD_DIM), lambda b, h: (b, off + h))

    def prev(off):
        return pl.BlockSpec((None, None, SUBLANES, HEAD_DIM), lambda b, h: (lp, b, 0, off + h))

    def taps(off):
        return pl.BlockSpec((CONV_W, HEAD_DIM), lambda b, h: (0, off + h))

    row_spec = pl.BlockSpec((1, HEAD_DIM), lambda b, h: (0, 0))
    return pl.pallas_call(
        functools.partial(_deltanet_kernel, t_rows=t_rows, t_valid=t_valid),
        grid=(n_seq, N_HEADS),
        in_specs=[col(QDN_BLK), col(KDN_BLK), col(VDN_BLK), col(Z_BLK),
                  pl.BlockSpec((t_rows, HEAD_DIM), lambda b, h: (b, 0)),
                  prev(QDN_BLK), prev(KDN_BLK), prev(VDN_BLK),
                  taps(QDN_BLK), taps(KDN_BLK), taps(VDN_BLK),
                  row_spec, row_spec, row_spec,
                  pl.BlockSpec((None, None, None, HEAD_DIM, HEAD_DIM), lambda b, h: (lp, b, h, 0, 0))],
        out_specs=[pl.BlockSpec((t_rows, HEAD_DIM), lambda b, h: (b, h)),
                   pl.BlockSpec((None, None, HEAD_DIM, HEAD_DIM), lambda b, h: (b, h, 0, 0))],
        out_shape=[jax.ShapeDtypeStruct((n_seq * t_rows, GROUP_W), BF16),
                   jax.ShapeDtypeStruct((n_seq, N_HEADS, HEAD_DIM, HEAD_DIM), F32)],
        compiler_params=_cparams("parallel", "arbitrary"),
        name="deltanet",
    )(proj, proj, proj, proj, proj_ab, conv_prev, conv_prev, conv_prev, conv_w, conv_w, conv_w,
      alog_row, dtb_row, gn_row, s0)


def _rel_bias_tile(tbl_ref, h, dist):
    val = jnp.full(dist.shape, tbl_ref[h, 0], F32)
    for b in range(1, REL_BUCKETS):
        val = jnp.where(dist >= BUCKET_THR[b], tbl_ref[h, b], val)
    return val


def _topk_mask(gate, n_cand, lane):
    rank = jnp.zeros(gate.shape, F32)
    for m in range(n_cand):
        gm = gate[:, m:m + 1]
        ahead = (gm > gate) | ((gm == gate) & (m < lane))
        rank = rank + ahead.astype(F32)
    return rank


def _moba_prompt_kernel(tbl_ref, q_ref, k_ref, v_ref, o_ref, kmean_scr, bown_scr, bnear_scr, m_scr, l_scr, acc_scr,
                        *, n_blocks):
    blk = MOBA_BLOCK
    h = pl.program_id(0)
    b = pl.program_id(1)
    j = pl.program_id(2)
    scale = HEAD_DIM ** -0.5

    @pl.when((b == 0) & (j == 0))
    def _():
        qi = lax.broadcasted_iota(jnp.int32, (blk, blk), 0)
        ki = lax.broadcasted_iota(jnp.int32, (blk, blk), 1)
        d = qi - ki
        bown_scr[...] = jnp.where(d >= 0, _rel_bias_tile(tbl_ref, h, d), -jnp.inf)
        bnear_scr[...] = _rel_bias_tile(tbl_ref, h, d + blk)

    @pl.when(j == 0)
    def _():
        kmean_scr[...] = jnp.zeros(kmean_scr.shape, F32)
        for n in range(n_blocks):
            kmean_scr[n:n + 1, :] = jnp.sum(k_ref[n * blk:(n + 1) * blk, :], axis=0, keepdims=True) * (1.0 / blk)

    q = q_ref[...]
    lane = lax.broadcasted_iota(jnp.int32, (blk, HEAD_DIM), 1)
    gate = jnp.where(lane < j, _dot_nt(q, kmean_scr[...], HI), -jnp.inf)
    rank = _topk_mask(gate, n_blocks, lane)
    selm = ((rank < MOBA_TOPK) & (lane < j)).astype(F32)
    qb = q.astype(BF16)

    m_scr[...] = jnp.full(m_scr.shape, -jnp.inf, F32)
    l_scr[...] = jnp.zeros(l_scr.shape, F32)
    acc_scr[...] = jnp.zeros(acc_scr.shape, F32)

    def attend(n, bias, selected):
        r0 = pl.multiple_of(n * blk, blk)
        kb = k_ref[pl.ds(r0, blk), :].astype(BF16)
        vb = v_ref[pl.ds(r0, blk), :].astype(BF16)
        s = _dot_nt(qb, kb) * scale + bias
        if selected:
            sel_col = jnp.sum(jnp.where(lane == n, selm, 0.0), axis=1, keepdims=True)
            s = jnp.where(sel_col > 0.0, s, -jnp.inf)
        m_prev = m_scr[...]
        m_new = jnp.maximum(m_prev, jnp.max(s, axis=1, keepdims=True))
        alpha = jnp.exp(m_prev - m_new)
        p = jnp.exp(s - m_new)
        l_scr[...] = alpha * l_scr[...] + jnp.sum(p, axis=1, keepdims=True)
        acc_scr[...] = alpha * acc_scr[...] + _dot(p.astype(BF16), vb)
        m_scr[...] = m_new

    attend(j, bown_scr[...], False)

    @pl.when(j >= 1)
    def _():
        attend(j - 1, bnear_scr[...], True)

    far_bias = tbl_ref[h, REL_BUCKETS - 1]

    def far(n, carry):
        attend(n, far_bias, True)
        return carry

    lax.fori_loop(0, jnp.maximum(j - 1, 0), far, 0)
    o_ref[...] = (acc_scr[...] / l_scr[...]).astype(o_ref.dtype)


def _moba_prompt(tbl, proj, n_seq, t_len):
    blk = MOBA_BLOCK
    n_blocks = t_len // blk
    assert MOBA_BLOCK >= BUCKET_THR[-1] and n_blocks <= HEAD_DIM
    return pl.pallas_call(
        functools.partial(_moba_prompt_kernel, n_blocks=n_blocks),
        grid=(N_HEADS, n_seq, n_blocks),
        in_specs=[pl.BlockSpec(memory_space=pltpu.SMEM),
                  pl.BlockSpec((blk, HEAD_DIM), lambda h, b, j: (b * n_blocks + j, QMB_BLK + h)),
                  pl.BlockSpec((t_len, HEAD_DIM), lambda h, b, j: (b, KMB_BLK + h)),
                  pl.BlockSpec((t_len, HEAD_DIM), lambda h, b, j: (b, VMB_BLK + h))],
        out_specs=pl.BlockSpec((blk, HEAD_DIM), lambda h, b, j: (b * n_blocks + j, h)),
        out_shape=jax.ShapeDtypeStruct((n_seq * t_len, GROUP_W), BF16),
        scratch_shapes=[pltpu.VMEM((HEAD_DIM, HEAD_DIM), F32),
                        pltpu.VMEM((blk, blk), F32), pltpu.VMEM((blk, blk), F32),
                        pltpu.VMEM((blk, 1), F32), pltpu.VMEM((blk, 1), F32), pltpu.VMEM((blk, HEAD_DIM), F32)],
        compiler_params=_cparams("arbitrary", "arbitrary", "arbitrary"),
        name="moba_prompt",
    )(tbl, proj, proj, proj)


PAGES_PER_STEP = 16


def _kmean_cache_kernel(pt_ref, *refs):
    del pt_ref
    pages, o_ref = refs[:PAGES_PER_STEP], refs[PAGES_PER_STEP]
    pages_per_block = PAGES_PER_STEP // SUBLANES
    for n in range(SUBLANES):
        tot = jnp.sum(pages[n * pages_per_block][...], axis=0, keepdims=True)
        for p in range(1, pages_per_block):
            tot = tot + jnp.sum(pages[n * pages_per_block + p][...], axis=0, keepdims=True)
        o_ref[n:n + 1, :] = tot * (1.0 / MOBA_BLOCK)


def _kmean_cache(cache_k2, page_table):
    n_layers, _, page, width = cache_k2.shape
    n_seq, n_pages = page_table.shape
    assert MOBA_BLOCK % page == 0 and PAGES_PER_STEP * page == SUBLANES * MOBA_BLOCK
    n_past_blocks = n_pages * page // MOBA_BLOCK

    def page_spec(p):
        return pl.BlockSpec((None, None, page, width),
                            lambda l, b, n, pt: (l, pt[b, n * PAGES_PER_STEP + p], 0, 0))

    return pl.pallas_call(
        _kmean_cache_kernel,
        grid_spec=pltpu.PrefetchScalarGridSpec(
            num_scalar_prefetch=1,
            grid=(n_layers, n_seq, n_pages // PAGES_PER_STEP),
            in_specs=[page_spec(p) for p in range(PAGES_PER_STEP)],
            out_specs=pl.BlockSpec((None, None, SUBLANES, width), lambda l, b, n, pt: (l, b, n, 0)),
        ),
        out_shape=jax.ShapeDtypeStruct((n_layers, n_seq, n_past_blocks, width), F32),
        compiler_params=_cparams("arbitrary", "arbitrary", "arbitrary"),
        name="kmean_cache",
    )(page_table, *([cache_k2] * PAGES_PER_STEP))


def _sample_select_kernel(q_ref, km_ref, o_ref, *, n_past_blocks):
    q = q_ref[...]
    km = km_ref[...]
    km = jnp.concatenate([km, jnp.zeros((HEAD_DIM - n_past_blocks, HEAD_DIM), F32)], axis=0)
    lane = lax.broadcasted_iota(jnp.int32, (SAMPLE_ROWS, HEAD_DIM), 1)
    gate = jnp.where(lane < n_past_blocks, _dot_nt(q, km, HI), -jnp.inf)
    rank = _topk_mask(gate, n_past_blocks, lane)
    lane_f = lane.astype(F32)
    out = jnp.zeros((SAMPLE_ROWS, HEAD_DIM), F32)
    for r in range(MOBA_TOPK):
        idx = jnp.sum(jnp.where(rank == r, lane_f, 0.0), axis=1, keepdims=True)
        out = jnp.where(lane == r, idx, out)
    o_ref[...] = out.astype(jnp.int32)


def _sample_select(proj, kmean, layer, n_seq):
    n_past_blocks = kmean.shape[2]
    assert MOBA_TOPK <= n_past_blocks <= HEAD_DIM
    return pl.pallas_call(
        functools.partial(_sample_select_kernel, n_past_blocks=n_past_blocks),
        grid=(n_seq, N_HEADS),
        in_specs=[pl.BlockSpec((SAMPLE_ROWS, HEAD_DIM), lambda b, h: (b, QMB_BLK + h)),
                  pl.BlockSpec((None, None, n_past_blocks, HEAD_DIM), lambda b, h: (layer, b, 0, h))],
        out_specs=pl.BlockSpec((None, None, SAMPLE_ROWS, HEAD_DIM), lambda b, h: (b, h, 0, 0)),
        out_shape=jax.ShapeDtypeStruct((n_seq, N_HEADS, SAMPLE_ROWS, HEAD_DIM), jnp.int32),
        compiler_params=_cparams("parallel", "arbitrary"),
        name="sample_select",
    )(proj, kmean)


def _moba_sample_kernel(pt_ref, sel_ref, tbl_ref, q_ref, kn_ref, vn_ref, *refs, n_valid, past_len, pages_per_block):
    del pt_ref
    n_pages = MOBA_TOPK * pages_per_block
    kp, vp, o_ref = refs[:n_pages], refs[n_pages:2 * n_pages], refs[2 * n_pages]
    b, h, i = pl.program_id(0), pl.program_id(1), pl.program_id(2)
    scale = HEAD_DIM ** -0.5
    blk = MOBA_BLOCK
    qb = q_ref[...].astype(BF16)
    qrow = lax.broadcasted_iota(jnp.int32, (SAMPLE_ROWS, blk), 0)
    koff = lax.broadcasted_iota(jnp.int32, (SAMPLE_ROWS, blk), 1)

    logits, values = [], []
    for s in range(MOBA_TOPK):
        sel = sel_ref[((b * N_HEADS + h) * n_valid + i) * MOBA_TOPK + s]
        ks = jnp.concatenate([kp[s * pages_per_block + p][...] for p in range(pages_per_block)], axis=0)
        vs = jnp.concatenate([vp[s * pages_per_block + p][...] for p in range(pages_per_block)], axis=0)
        dist = (past_len + qrow) - (sel * blk + koff)
        logits.append(_dot_nt(qb, ks.astype(BF16)) * scale + _rel_bias_tile(tbl_ref, h, dist))
        values.append(vs.astype(BF16))

    pad = jnp.zeros((HEAD_DIM - SAMPLE_ROWS, HEAD_DIM), F32)
    kn = jnp.concatenate([kn_ref[...], pad], axis=0).astype(BF16)
    vn = jnp.concatenate([vn_ref[...], pad], axis=0).astype(BF16)
    d_own = qrow[:, :HEAD_DIM] - koff[:, :HEAD_DIM]
    l_own = _dot_nt(qb, kn) * scale + _rel_bias_tile(tbl_ref, h, d_own)
    l_own = jnp.where(d_own >= 0, l_own, -jnp.inf)

    m = jnp.max(l_own, axis=1, keepdims=True)
    for l_s in logits:
        m = jnp.maximum(m, jnp.max(l_s, axis=1, keepdims=True))
    p_own = jnp.exp(l_own - m)
    denom = jnp.sum(p_own, axis=1, keepdims=True)
    acc = _dot(p_own.astype(BF16), vn)
    for l_s, v_s in zip(logits, values):
        p_s = jnp.exp(l_s - m)
        denom = denom + jnp.sum(p_s, axis=1, keepdims=True)
        acc = acc + _dot(p_s.astype(BF16), v_s)
    out = acc / denom

    @pl.when(i == 0)
    def _():
        o_ref[...] = jnp.zeros(o_ref.shape, o_ref.dtype)

    orow = lax.broadcasted_iota(jnp.int32, (SAMPLE_ROWS, HEAD_DIM), 0)
    o_ref[...] = jnp.where(orow == i, out.astype(o_ref.dtype), o_ref[...])


def _moba_sample(page_table, sel_flat, tbl, proj, cache_k2, cache_v2, layer, n_seq, n_valid):
    page = cache_k2.shape[2]
    pages_per_block = MOBA_BLOCK // page
    past_len = page_table.shape[1] * page
    assert past_len % MOBA_BLOCK == 0
    n_pages = MOBA_TOPK * pages_per_block

    def page_spec(s, p):
        def index(b, h, i, pt, sel):
            blk_idx = sel[((b * N_HEADS + h) * n_valid + i) * MOBA_TOPK + s]
            return (layer, pt[b, blk_idx * pages_per_block + p], 0, h)
        return pl.BlockSpec((None, None, page, HEAD_DIM), index)

    page_specs = [page_spec(s, p) for s in range(MOBA_TOPK) for p in range(pages_per_block)]

    def new_rows(off):
        return pl.BlockSpec((SAMPLE_ROWS, HEAD_DIM), lambda b, h, i, pt, sel: (b, off + h))

    return pl.pallas_call(
        functools.partial(_moba_sample_kernel, n_valid=n_valid, past_len=past_len,
                          pages_per_block=pages_per_block),
        grid_spec=pltpu.PrefetchScalarGridSpec(
            num_scalar_prefetch=2,
            grid=(n_seq, N_HEADS, n_valid),
            in_specs=[pl.BlockSpec(memory_space=pltpu.SMEM),
                      new_rows(QMB_BLK), new_rows(KMB_BLK), new_rows(VMB_BLK)] + page_specs + page_specs,
            out_specs=pl.BlockSpec((SAMPLE_ROWS, HEAD_DIM), lambda b, h, i, pt, sel: (b, h)),
        ),
        out_shape=jax.ShapeDtypeStruct((n_seq * SAMPLE_ROWS, GROUP_W), BF16),
        compiler_params=_cparams("arbitrary", "arbitrary", "arbitrary"),
        name="moba_sample",
    )(page_table, sel_flat, tbl, proj, proj, proj, *([cache_k2] * n_pages), *([cache_v2] * n_pages))


def _pad_row(v):
    return jnp.pad(v.astype(F32), (0, HEAD_DIM - v.shape[0])).reshape(1, HEAD_DIM)


def kernel(x_prompt, x_sample, cache_k, cache_v, page_table, state_conv, state_delta, w_in, conv_w, a_log, dt_bias,
           gn_w, w_out, norm_mix, norm_ffn, w_gate, w_up, w_down, rel_bias, final_norm):
    n_layers = w_in.shape[0]
    bp, t_len, d_model = x_prompt.shape
    bs, t_dec, _ = x_sample.shape
    n_pool, page = cache_k.shape[1], cache_k.shape[2]
    conv_ch = 3 * GROUP_W
    a_off = conv_ch + GROUP_W
    mb_off = a_off + 2 * N_HEADS
    assert t_len % MOBA_BLOCK == 0 and CONV_W - 1 <= t_dec <= SAMPLE_ROWS

    xp = x_prompt.reshape(bp * t_len, d_model)
    xs = jnp.pad(x_sample, ((0, 0), (0, SAMPLE_ROWS - t_dec), (0, 0))).reshape(bs * SAMPLE_ROWS, d_model)
    tm_p, tm_s = 512, bs * SAMPLE_ROWS

    cache_k2 = cache_k.reshape(n_layers, n_pool, page, GROUP_W)
    cache_v2 = cache_v.reshape(n_layers, n_pool, page, GROUP_W)
    kmean = _kmean_cache(cache_k2, page_table)
    tbl = rel_bias.astype(F32).T

    conv_prev_p = jnp.zeros((1, bp, SUBLANES, conv_ch), F32)
    conv_prev_s = jnp.pad(state_conv.astype(F32), ((0, 0), (0, 0), (SUBLANES - (CONV_W - 1), 0), (0, 0)))
    s0_p = jnp.zeros((1, bp, N_HEADS, HEAD_DIM, HEAD_DIM), F32)
    s0_s = state_delta.astype(F32)

    outs = {name: [] for name in ("kp", "vp", "cp", "sp", "kd", "vd", "cd", "sd")}
    for l in range(n_layers):
        w_main = jnp.concatenate([w_in[l][:, :a_off], w_in[l][:, mb_off:]], axis=1).astype(BF16)
        w_ab = jnp.pad(w_in[l][:, a_off:mb_off], ((0, 0), (0, HEAD_DIM - 2 * N_HEADS))).astype(BF16)
        w_out_dn = w_out[l][:GROUP_W].astype(BF16)
        w_out_mb = w_out[l][GROUP_W:].astype(BF16)
        wg, wu, wd = w_gate[l].astype(BF16), w_up[l].astype(BF16), w_down[l].astype(BF16)
        nm, nf = norm_mix[l].reshape(1, d_model), norm_ffn[l].reshape(1, d_model)
        alog_row, dtb_row, gn_row = _pad_row(a_log[l]), _pad_row(dt_bias[l]), gn_w[l].reshape(1, HEAD_DIM)

        proj, proj_ab = _inproj(xp, nm, w_main, w_ab, tm_p)
        mix_dn, s_new = _deltanet(proj, proj_ab, conv_prev_p, conv_w[l], alog_row, dtb_row, gn_row, s0_p, l,
                                  bp, t_len, t_len)
        mix_mb = _moba_prompt(tbl, proj, bp, t_len)
        xp = _outproj(xp, mix_dn, mix_mb, w_out_dn, w_out_mb, tm_p)
        xp = _ffn_down(xp, _ffn_up(xp, nf, wg, wu, tm_p), wd, tm_p)
        proj3 = proj.reshape(bp, t_len, MAIN_COLS)
        outs["kp"].append(proj3[:, :, KMB_BLK * HEAD_DIM:VMB_BLK * HEAD_DIM].reshape(bp, t_len, N_HEADS, HEAD_DIM))
        outs["vp"].append(proj3[:, :, VMB_BLK * HEAD_DIM:].reshape(bp, t_len, N_HEADS, HEAD_DIM))
        outs["cp"].append(proj3[:, t_len - (CONV_W - 1):, :conv_ch])
        outs["sp"].append(s_new)

        proj, proj_ab = _inproj(xs, nm, w_main, w_ab, tm_s)
        mix_dn, s_new = _deltanet(proj, proj_ab, conv_prev_s, conv_w[l], alog_row, dtb_row, gn_row, s0_s, l,
                                  bs, SAMPLE_ROWS, t_dec)
        sel = _sample_select(proj, kmean, l, bs)
        sel_flat = sel[:, :, :t_dec, :MOBA_TOPK].reshape(-1)
        mix_mb = _moba_sample(page_table, sel_flat, tbl, proj, cache_k2, cache_v2, l, bs, t_dec)
        xs = _outproj(xs, mix_dn, mix_mb, w_out_dn, w_out_mb, tm_s)
        xs = _ffn_down(xs, _ffn_up(xs, nf, wg, wu, tm_s), wd, tm_s)
        proj3 = proj.reshape(bs, SAMPLE_ROWS, MAIN_COLS)
        outs["kd"].append(proj3[:, :t_dec, KMB_BLK * HEAD_DIM:VMB_BLK * HEAD_DIM].reshape(bs, t_dec, N_HEADS, HEAD_DIM))
        outs["vd"].append(proj3[:, :t_dec, VMB_BLK * HEAD_DIM:].reshape(bs, t_dec, N_HEADS, HEAD_DIM))
        outs["cd"].append(proj3[:, t_dec - (CONV_W - 1):t_dec, :conv_ch])
        outs["sd"].append(s_new)

    fn = final_norm.reshape(1, d_model)
    y_prompt = _final_norm(xp, fn, tm_p).reshape(bp, t_len, d_model)
    y_sample = _final_norm(xs, fn, tm_s).reshape(bs, SAMPLE_ROWS, d_model)[:, :t_dec]
    return (y_prompt, y_sample) + tuple(jnp.stack(outs[n]) for n in ("kp", "vp", "cp", "sp", "kd", "vd", "cd", "sd"))
```

```python
import functools
import math

import jax
import jax.numpy as jnp
import numpy as np
from jax import lax
from jax.experimental import pallas as pl
from jax.experimental.pallas import tpu as pltpu

F32 = jnp.float32
BF16 = jnp.bfloat16
HI = lax.Precision.HIGHEST

HEAD_DIM = 128
N_HEADS = 8
GROUP_W = N_HEADS * HEAD_DIM
CONV_W = 4
DN_CHUNK = 128
MOBA_BLOCK = 256
MOBA_TOPK = 3
REL_BUCKETS = 32
REL_MAX_DIST = 128
RMS_EPS = 1e-6
L2_EPS = 1e-6
SUBLANES = 8
SAMPLE_ROWS = 8

QDN_BLK, KDN_BLK, VDN_BLK, Z_BLK, QMB_BLK, KMB_BLK, VMB_BLK = 0, 8, 16, 24, 32, 40, 48
MAIN_COLS = 56 * HEAD_DIM

TM_PROMPT = 512
TN_INPROJ = 1024
TN_OUTPROJ = 1024
TN_FFN_UP = 512
TN_FFN_DOWN = 512
VMEM_LIMIT = 56 * 1024 * 1024


def _bucket_thresholds():
    d = np.arange(0, 4 * REL_MAX_DIST)
    max_exact = REL_BUCKETS // 2
    nf = np.maximum(d, 1).astype(np.float32)
    large = max_exact + (np.log(nf / np.float32(max_exact)) / np.float32(math.log(REL_MAX_DIST / max_exact))
                         * (REL_BUCKETS - max_exact)).astype(np.int32)
    bucket = np.where(d < max_exact, d, np.minimum(large, REL_BUCKETS - 1))
    return tuple(int(np.argmax(bucket >= b)) for b in range(REL_BUCKETS))


BUCKET_THR = _bucket_thresholds()


def _cparams(*sem):
    return pltpu.CompilerParams(dimension_semantics=sem, vmem_limit_bytes=VMEM_LIMIT)


def _rms_rows(x, w_row):
    return x * lax.rsqrt(jnp.mean(x * x, axis=-1, keepdims=True) + RMS_EPS) * w_row


def _dot(a, b, precision=None):
    return jnp.dot(a, b, preferred_element_type=F32, precision=precision)


def _dot_nt(a, b, precision=None):
    return lax.dot_general(a, b, (((1,), (1,)), ((), ())), preferred_element_type=F32, precision=precision)


def _split_bf16(a):
    hi = a.astype(BF16)
    return hi, (a - hi.astype(F32)).astype(BF16)


def _dot3(a_parts, b_parts):
    (ah, al), (bh, bl) = a_parts, b_parts
    return _dot(ah, bh) + (_dot(ah, bl) + _dot(al, bh))


def _inproj_kernel(x_ref, nw_ref, w_ref, wab_ref, o_ref, oab_ref, h_scr):
    @pl.when(pl.program_id(1) == 0)
    def _():
        hb = _rms_rows(x_ref[...], nw_ref[...]).astype(BF16)
        h_scr[...] = hb
        oab_ref[...] = _dot(hb, wab_ref[...])

    o_ref[...] = _dot(h_scr[...], w_ref[...])


def _inproj(x, nw, w_main, w_ab, tm):
    m, d = x.shape
    n = w_main.shape[1]
    tn = TN_INPROJ
    return pl.pallas_call(
        _inproj_kernel,
        grid=(m // tm, n // tn),
        in_specs=[pl.BlockSpec((tm, d), lambda i, j: (i, 0)),
                  pl.BlockSpec((1, d), lambda i, j: (0, 0)),
                  pl.BlockSpec((d, tn), lambda i, j: (0, j)),
                  pl.BlockSpec((d, HEAD_DIM), lambda i, j: (0, 0))],
        out_specs=[pl.BlockSpec((tm, tn), lambda i, j: (i, j)),
                   pl.BlockSpec((tm, HEAD_DIM), lambda i, j: (i, 0))],
        out_shape=[jax.ShapeDtypeStruct((m, n), F32), jax.ShapeDtypeStruct((m, HEAD_DIM), F32)],
        scratch_shapes=[pltpu.VMEM((tm, d), BF16)],
        compiler_params=_cparams("parallel", "arbitrary"),
        name="inproj",
    )(x, nw, w_main, w_ab)


def _outproj_kernel(x_ref, a_ref, b_ref, wa_ref, wb_ref, o_ref):
    o_ref[...] = x_ref[...] + (_dot(a_ref[...], wa_ref[...]) + _dot(b_ref[...], wb_ref[...]))


def _outproj(x, mix_dn, mix_mb, w_dn, w_mb, tm):
    m, d = x.shape
    k = mix_dn.shape[1]
    tn = TN_OUTPROJ
    return pl.pallas_call(
        _outproj_kernel,
        grid=(m // tm, d // tn),
        in_specs=[pl.BlockSpec((tm, tn), lambda i, j: (i, j)),
                  pl.BlockSpec((tm, k), lambda i, j: (i, 0)),
                  pl.BlockSpec((tm, k), lambda i, j: (i, 0)),
                  pl.BlockSpec((k, tn), lambda i, j: (0, j)),
                  pl.BlockSpec((k, tn), lambda i, j: (0, j))],
        out_specs=pl.BlockSpec((tm, tn), lambda i, j: (i, j)),
        out_shape=jax.ShapeDtypeStruct((m, d), F32),
        compiler_params=_cparams("parallel", "arbitrary"),
        name="outproj",
    )(x, mix_dn, mix_mb, w_dn, w_mb)


def _ffn_up_kernel(x_ref, nw_ref, wg_ref, wu_ref, o_ref, h_scr):
    @pl.when(pl.program_id(1) == 0)
    def _():
        h_scr[...] = _rms_rows(x_ref[...], nw_ref[...]).astype(BF16)

    h = h_scr[...]
    g = _dot(h, wg_ref[...])
    u = _dot(h, wu_ref[...])
    o_ref[...] = (g * jax.nn.sigmoid(g) * u).astype(BF16)


def _ffn_up(x, nw, wg, wu, tm):
    m, d = x.shape
    n = wg.shape[1]
    tn = TN_FFN_UP
    return pl.pallas_call(
        _ffn_up_kernel,
        grid=(m // tm, n // tn),
        in_specs=[pl.BlockSpec((tm, d), lambda i, j: (i, 0)),
                  pl.BlockSpec((1, d), lambda i, j: (0, 0)),
                  pl.BlockSpec((d, tn), lambda i, j: (0, j)),
                  pl.BlockSpec((d, tn), lambda i, j: (0, j))],
        out_specs=pl.BlockSpec((tm, tn), lambda i, j: (i, j)),
        out_shape=jax.ShapeDtypeStruct((m, n), BF16),
        scratch_shapes=[pltpu.VMEM((tm, d), BF16)],
        compiler_params=_cparams("parallel", "arbitrary"),
        name="ffn_up",
    )(x, nw, wg, wu)


def _ffn_down_kernel(x_ref, a_ref, w_ref, o_ref):
    o_ref[...] = x_ref[...] + _dot(a_ref[...], w_ref[...])


def _ffn_down(x, act, wd, tm):
    m, d = x.shape
    k = act.shape[1]
    tn = TN_FFN_DOWN
    return pl.pallas_call(
        _ffn_down_kernel,
        grid=(m // tm, d // tn),
        in_specs=[pl.BlockSpec((tm, tn), lambda i, j: (i, j)),
                  pl.BlockSpec((tm, k), lambda i, j: (i, 0)),
                  pl.BlockSpec((k, tn), lambda i, j: (0, j))],
        out_specs=pl.BlockSpec((tm, tn), lambda i, j: (i, j)),
        out_shape=jax.ShapeDtypeStruct((m, d), F32),
        compiler_params=_cparams("parallel", "arbitrary"),
        name="ffn_down",
    )(x, act, wd)


def _final_norm_kernel(x_ref, nw_ref, o_ref):
    o_ref[...] = _rms_rows(x_ref[...], nw_ref[...])


def _final_norm(x, nw, tm):
    m, d = x.shape
    return pl.pallas_call(
        _final_norm_kernel,
        grid=(m // tm,),
        in_specs=[pl.BlockSpec((tm, d), lambda i: (i, 0)), pl.BlockSpec((1, d), lambda i: (0, 0))],
        out_specs=pl.BlockSpec((tm, d), lambda i: (i, 0)),
        out_shape=jax.ShapeDtypeStruct((m, d), F32),
        compiler_params=_cparams("parallel"),
        name="final_norm",
    )(x, nw)


def _deltanet_kernel(q_ref, k_ref, v_ref, z_ref, ab_ref, pq_ref, pk_ref, pv_ref, cq_ref, ck_ref, cv_ref,
                     alog_ref, dtb_ref, gn_ref, s0_ref, o_ref, s_ref,
                     u_scr, wq_scr, kdt_scr, attn_scr, egl_scr, *, t_rows, t_valid):
    c = DN_CHUNK
    h = pl.program_id(1)
    n_chunks = max(t_rows // c, 1)
    rows = min(t_rows, c)
    n_doublings = c.bit_length() - 2

    ii = lax.broadcasted_iota(jnp.int32, (c, c), 0)
    jj = lax.broadcasted_iota(jnp.int32, (c, c), 1)
    tril = ii >= jj
    strict = ii > jj
    tril_f = tril.astype(F32)
    eye_f = (ii == jj).astype(F32)
    ones_cc = jnp.ones((c, c), F32)
    lane = lax.broadcasted_iota(jnp.int32, (c, HEAD_DIM), 1)
    row = lax.broadcasted_iota(jnp.int32, (c, HEAD_DIM), 0)
    neg_decay_rate = -jnp.exp(alog_ref[...])
    dtb = dtb_ref[...]
    gn = gn_ref[...]
    cq, ck, cv = cq_ref[...], ck_ref[...], cv_ref[...]

    def load(ref, r0):
        x = ref[pl.ds(r0, rows), :]
        if rows < c:
            x = jnp.concatenate([x, jnp.zeros((c - rows, HEAD_DIM), F32)], axis=0)
        return x

    def conv_silu(ref, prev_ref, ci, r0, taps):
        x = load(ref, r0)
        tail = prev_ref[...]
        if n_chunks > 1:
            before = ref[pl.ds(pl.multiple_of(jnp.maximum(r0 - SUBLANES, 0), SUBLANES), SUBLANES), :]
            tail = jnp.where(ci == 0, tail, before)
        win = jnp.concatenate([tail, x], axis=0)
        y = win[5:5 + c] * taps[0:1]
        for i in range(1, CONV_W):
            y = y + win[5 + i:5 + i + c] * taps[i:i + 1]
        return y * jax.nn.sigmoid(y)

    def l2n(x):
        return x * lax.rsqrt(jnp.sum(x * x, axis=-1, keepdims=True) + L2_EPS)

    def prepare(ci, carry):
        r0 = pl.multiple_of(ci * rows, rows)
        c0 = pl.multiple_of(ci * c, c)
        q = l2n(conv_silu(q_ref, pq_ref, ci, r0, cq)) * (HEAD_DIM ** -0.5)
        k = l2n(conv_silu(k_ref, pk_ref, ci, r0, ck))
        v = conv_silu(v_ref, pv_ref, ci, r0, cv)

        ab = load(ab_ref, r0)
        z_ab = ab + dtb
        softplus = jnp.maximum(z_ab, 0.0) + jnp.log1p(jnp.exp(-jnp.abs(z_ab)))
        g_sel, b_sel = lane == h, lane == N_HEADS + h
        if t_valid < n_chunks * c:
            valid = (row + ci * c) < t_valid
            g_sel, b_sel = g_sel & valid, b_sel & valid
        g_all = jnp.where(g_sel, neg_decay_rate * softplus, 0.0)
        beta = jnp.sum(jnp.where(b_sel, jax.nn.sigmoid(ab), 0.0), axis=1, keepdims=True)
        gc = jnp.sum(_dot(tril_f, g_all, HI), axis=1, keepdims=True)
        gc_row = _dot(ones_cc, eye_f * gc, HI)
        decay = jnp.where(tril, jnp.exp(jnp.where(tril, gc - gc_row, 0.0)), 0.0)
        e_gc = jnp.exp(gc)
        g_last = gc[c - 1:c, :]

        kb = k * beta
        vb = v * beta
        k16 = k.astype(BF16)
        neg_l = jnp.where(strict, -(_dot_nt(kb.astype(BF16), k16) * decay), 0.0)
        tinv = eye_f + neg_l
        p = _split_bf16(neg_l)
        for r in range(n_doublings):
            p_next = _dot3(p, p)
            p = _split_bf16(p_next)
            tinv = tinv + _dot3(_split_bf16(tinv), p)
        uw = _dot3(_split_bf16(tinv), _split_bf16(jnp.concatenate([vb, kb * e_gc], axis=1)))
        u_scr[pl.ds(c0, c), :] = uw[:, :HEAD_DIM]
        wq_scr[pl.ds(2 * c0, c), :] = uw[:, HEAD_DIM:].astype(BF16)
        wq_scr[pl.ds(2 * c0 + c, c), :] = (q * e_gc).astype(BF16)
        attn_scr[pl.ds(c0, c), :] = jnp.where(tril, _dot_nt(q.astype(BF16), k16) * decay, 0.0).astype(BF16)
        kdt_scr[pl.ds(c0, c), :] = (k * jnp.exp(g_last - gc)).T.astype(BF16)
        egl_scr[pl.ds(pl.multiple_of(ci * SUBLANES, SUBLANES), SUBLANES), :] = jnp.broadcast_to(
            jnp.exp(g_last), (SUBLANES, HEAD_DIM))
        return carry

    lax.fori_loop(0, n_chunks, prepare, 0, unroll=min(2, n_chunks))

    def recur(ci, s):
        r0 = pl.multiple_of(ci * rows, rows)
        c0 = pl.multiple_of(ci * c, c)
        r = _dot(wq_scr[pl.ds(2 * c0, 2 * c), :], s.astype(BF16))
        v_new = (u_scr[pl.ds(c0, c), :] - r[:c]).astype(BF16)
        o = r[c:] + _dot(attn_scr[pl.ds(c0, c), :], v_new)
        e_last = egl_scr[pl.ds(pl.multiple_of(ci * SUBLANES, SUBLANES), SUBLANES), :][0:1]
        s_new = s * e_last + _dot(kdt_scr[pl.ds(c0, c), :], v_new)
        on = o * lax.rsqrt(jnp.mean(o * o, axis=-1, keepdims=True) + RMS_EPS) * gn
        zc = load(z_ref, r0)
        out = (on * (zc * jax.nn.sigmoid(zc))).astype(o_ref.dtype)
        o_ref[pl.ds(r0, rows), :] = out[:rows]
        return s_new

    s_ref[...] = lax.fori_loop(0, n_chunks, recur, s0_ref[...])


def _deltanet(proj, proj_ab, conv_prev, conv_w, alog_row, dtb_row, gn_row, s0, layer, n_seq, t_rows, t_valid):
    lp = layer if conv_prev.shape[0] > 1 else 0
    t_pad = max(t_rows // DN_CHUNK, 1) * DN_CHUNK
    n_chunks = t_pad // DN_CHUNK

    def col(off):
        return pl.BlockSpec((t_rows, HEAD_DIM), lambda b, h: (b, off + h))

    def prev(off):
        return pl.BlockSpec((None, None, SUBLANES, HEAD_DIM), lambda b, h: (lp, b, 0, off + h))

    def taps(off):
        return pl.BlockSpec((CONV_W, HEAD_DIM), lambda b, h: (0, off + h))

    row_spec = pl.BlockSpec((1, HEAD_DIM), lambda b, h: (0, 0))
    return pl.pallas_call(
        functools.partial(_deltanet_kernel, t_rows=t_rows, t_valid=t_valid),
        grid=(n_seq, N_HEADS),
        in_specs=[col(QDN_BLK), col(KDN_BLK), col(VDN_BLK), col(Z_BLK),
                  pl.BlockSpec((t_rows, HEAD_DIM), lambda b, h: (b, 0)),
                  prev(QDN_BLK), prev(KDN_BLK), prev(VDN_BLK),
                  taps(QDN_BLK), taps(KDN_BLK), taps(VDN_BLK),
                  row_spec, row_spec, row_spec,
                  pl.BlockSpec((None, None, None, HEAD_DIM, HEAD_DIM), lambda b, h: (lp, b, h, 0, 0))],
        out_specs=[pl.BlockSpec((t_rows, HEAD_DIM), lambda b, h: (b, h)),
                   pl.BlockSpec((None, None, HEAD_DIM, HEAD_DIM), lambda b, h: (b, h, 0, 0))],
        out_shape=[jax.ShapeDtypeStruct((n_seq * t_rows, GROUP_W), BF16),
                   jax.ShapeDtypeStruct((n_seq, N_HEADS, HEAD_DIM, HEAD_DIM), F32)],
        scratch_shapes=[pltpu.VMEM((t_pad, HEAD_DIM), F32),
                        pltpu.VMEM((2 * t_pad, HEAD_DIM), BF16),
                        pltpu.VMEM((t_pad, DN_CHUNK), BF16),
                        pltpu.VMEM((t_pad, DN_CHUNK), BF16),
                        pltpu.VMEM((n_chunks * SUBLANES, HEAD_DIM), F32)],
        compiler_params=_cparams("parallel", "arbitrary"),
        name="deltanet",
    )(proj, proj, proj, proj, proj_ab, conv_prev, conv_prev, conv_prev, conv_w, conv_w, conv_w,
      alog_row, dtb_row, gn_row, s0)


def _rel_bias_tile(tbl_ref, h, dist):
    val = jnp.full(dist.shape, tbl_ref[h, 0], F32)
    for b in range(1, REL_BUCKETS):
        val = jnp.where(dist >= BUCKET_THR[b], tbl_ref[h, b], val)
    return val


def _gate_rank(gate, n_cand, lane):
    rank = jnp.zeros(gate.shape, F32)
    for m in range(n_cand):
        gm = gate[:, m:m + 1]
        ahead = (gm > gate) | ((gm == gate) & (m < lane))
        rank = rank + ahead.astype(F32)
    return rank


def _moba_prompt_kernel(tbl_ref, q_ref, k_ref, v_ref, o_ref, kmean_scr, bown_scr, bnear_scr, k16_scr, v16_scr,
                        *, n_blocks):
    blk = MOBA_BLOCK
    h = pl.program_id(0)
    b = pl.program_id(1)
    j = pl.program_id(2)
    scale = HEAD_DIM ** -0.5

    @pl.when((b == 0) & (j == 0))
    def _():
        qi = lax.broadcasted_iota(jnp.int32, (blk, blk), 0)
        ki = lax.broadcasted_iota(jnp.int32, (blk, blk), 1)
        d = qi - ki
        bown_scr[...] = jnp.where(d >= 0, _rel_bias_tile(tbl_ref, h, d), -jnp.inf)
        bnear_scr[...] = _rel_bias_tile(tbl_ref, h, d + blk)

    @pl.when(j == 0)
    def _():
        kmean_scr[...] = jnp.zeros(kmean_scr.shape, F32)
        for n in range(n_blocks):
            kmean_scr[n:n + 1, :] = jnp.sum(k_ref[n * blk:(n + 1) * blk, :], axis=0, keepdims=True) * (1.0 / blk)
        k16_scr[...] = k_ref[...].astype(BF16)
        v16_scr[...] = v_ref[...].astype(BF16)

    far_bias = tbl_ref[h, REL_BUCKETS - 1]

    def attend(jb):
        q = q_ref[...]
        n_keys = (jb + 1) * blk
        s = _dot_nt(q.astype(BF16), k16_scr[0:n_keys, :]) * scale
        gated = jb > MOBA_TOPK
        if gated:
            lane = lax.broadcasted_iota(jnp.int32, (blk, HEAD_DIM), 1)
            gate = jnp.where(lane < jb, _dot_nt(q, kmean_scr[...], HI), -jnp.inf)
            keep = (_gate_rank(gate, jb, lane) < MOBA_TOPK).astype(F32)
        pieces = []
        for n in range(jb + 1):
            sn = s[:, n * blk:(n + 1) * blk]
            if n == jb:
                sn = sn + bown_scr[...]
            else:
                sn = sn + (bnear_scr[...] if n == jb - 1 else far_bias)
                if gated:
                    sn = jnp.where(keep[:, n:n + 1] > 0.0, sn, -jnp.inf)
            pieces.append(sn)
        m = jnp.max(pieces[0], axis=1, keepdims=True)
        for sn in pieces[1:]:
            m = jnp.maximum(m, jnp.max(sn, axis=1, keepdims=True))
        probs = [jnp.exp(sn - m) for sn in pieces]
        denom = jnp.sum(probs[0], axis=1, keepdims=True)
        for pn in probs[1:]:
            denom = denom + jnp.sum(pn, axis=1, keepdims=True)
        p = jnp.concatenate([pn.astype(BF16) for pn in probs], axis=1) if jb else probs[0].astype(BF16)
        o_ref[...] = (_dot(p, v16_scr[0:n_keys, :]) / denom).astype(o_ref.dtype)

    for jb in range(n_blocks):
        pl.when(j == jb)(functools.partial(attend, jb))


def _moba_prompt(tbl, proj, n_seq, t_len):
    blk = MOBA_BLOCK
    n_blocks = t_len // blk
    assert MOBA_BLOCK >= BUCKET_THR[-1] and n_blocks <= HEAD_DIM
    return pl.pallas_call(
        functools.partial(_moba_prompt_kernel, n_blocks=n_blocks),
        grid=(N_HEADS, n_seq, n_blocks),
        in_specs=[pl.BlockSpec(memory_space=pltpu.SMEM),
                  pl.BlockSpec((blk, HEAD_DIM), lambda h, b, j: (b * n_blocks + j, QMB_BLK + h)),
                  pl.BlockSpec((t_len, HEAD_DIM), lambda h, b, j: (b, KMB_BLK + h)),
                  pl.BlockSpec((t_len, HEAD_DIM), lambda h, b, j: (b, VMB_BLK + h))],
        out_specs=pl.BlockSpec((blk, HEAD_DIM), lambda h, b, j: (b * n_blocks + j, h)),
        out_shape=jax.ShapeDtypeStruct((n_seq * t_len, GROUP_W), BF16),
        scratch_shapes=[pltpu.VMEM((HEAD_DIM, HEAD_DIM), F32),
                        pltpu.VMEM((blk, blk), F32), pltpu.VMEM((blk, blk), F32),
                        pltpu.VMEM((t_len, HEAD_DIM), BF16), pltpu.VMEM((t_len, HEAD_DIM), BF16)],
        compiler_params=_cparams("arbitrary", "arbitrary", "arbitrary"),
        name="moba_prompt",
    )(tbl, proj, proj, proj)


PAGES_PER_STEP = 16


def _kmean_cache_kernel(pt_ref, *refs):
    del pt_ref
    pages, o_ref = refs[:PAGES_PER_STEP], refs[PAGES_PER_STEP]
    pages_per_block = PAGES_PER_STEP // SUBLANES
    for n in range(SUBLANES):
        tot = jnp.sum(pages[n * pages_per_block][...], axis=0)
        for p in range(1, pages_per_block):
            tot = tot + jnp.sum(pages[n * pages_per_block + p][...], axis=0)
        o_ref[n] = tot * (1.0 / MOBA_BLOCK)


def _kmean_cache(cache_k, page_table):
    n_layers, _, page, n_heads, hd = cache_k.shape
    n_seq, n_pages = page_table.shape
    assert MOBA_BLOCK % page == 0 and PAGES_PER_STEP * page == SUBLANES * MOBA_BLOCK
    n_past_blocks = n_pages * page // MOBA_BLOCK

    def page_spec(p):
        return pl.BlockSpec((None, None, page, n_heads, hd),
                            lambda l, b, n, pt: (l, pt[b, n * PAGES_PER_STEP + p], 0, 0, 0))

    return pl.pallas_call(
        _kmean_cache_kernel,
        grid_spec=pltpu.PrefetchScalarGridSpec(
            num_scalar_prefetch=1,
            grid=(n_layers, n_seq, n_pages // PAGES_PER_STEP),
            in_specs=[page_spec(p) for p in range(PAGES_PER_STEP)],
            out_specs=pl.BlockSpec((None, None, SUBLANES, n_heads, hd), lambda l, b, n, pt: (l, b, n, 0, 0)),
        ),
        out_shape=jax.ShapeDtypeStruct((n_layers, n_seq, n_past_blocks, n_heads, hd), F32),
        compiler_params=_cparams("arbitrary", "arbitrary", "arbitrary"),
        name="kmean_cache",
    )(page_table, *([cache_k] * PAGES_PER_STEP))


def _sample_select_kernel(q_ref, km_ref, o_ref, *, n_past_blocks):
    q = q_ref[...]
    km = km_ref[...]
    km = jnp.concatenate([km, jnp.zeros((HEAD_DIM - n_past_blocks, HEAD_DIM), F32)], axis=0)
    lane = lax.broadcasted_iota(jnp.int32, (SAMPLE_ROWS, HEAD_DIM), 1)
    gate = jnp.where(lane < n_past_blocks, _dot_nt(q, km, HI), -jnp.inf)
    rank = _gate_rank(gate, n_past_blocks, lane)
    lane_f = lane.astype(F32)
    out = jnp.zeros((SAMPLE_ROWS, HEAD_DIM), F32)
    for r in range(MOBA_TOPK):
        idx = jnp.sum(jnp.where(rank == r, lane_f, 0.0), axis=1, keepdims=True)
        out = jnp.where(lane == r, idx, out)
    o_ref[...] = out.astype(jnp.int32)


def _sample_select(proj, kmean_t, layer, n_seq):
    n_past_blocks = kmean_t.shape[3]
    assert MOBA_TOPK <= n_past_blocks <= HEAD_DIM
    return pl.pallas_call(
        functools.partial(_sample_select_kernel, n_past_blocks=n_past_blocks),
        grid=(n_seq, N_HEADS),
        in_specs=[pl.BlockSpec((SAMPLE_ROWS, HEAD_DIM), lambda b, h: (b, QMB_BLK + h)),
                  pl.BlockSpec((None, None, None, n_past_blocks, HEAD_DIM), lambda b, h: (layer, b, h, 0, 0))],
        out_specs=pl.BlockSpec((None, None, SAMPLE_ROWS, HEAD_DIM), lambda b, h: (b, h, 0, 0)),
        out_shape=jax.ShapeDtypeStruct((n_seq, N_HEADS, SAMPLE_ROWS, HEAD_DIM), jnp.int32),
        compiler_params=_cparams("parallel", "arbitrary"),
        name="sample_select",
    )(proj, kmean_t)


def _moba_sample_kernel(pt_ref, sel_ref, tbl_ref, q_ref, kn_ref, vn_ref, ck_hbm, cv_hbm, o_ref, kbuf, vbuf, sem,
                        *, layer, n_valid, past_len, page):
    blk = MOBA_BLOCK
    pages_per_block = blk // page
    b, h = pl.program_id(0), pl.program_id(1)
    step = b * N_HEADS + h
    n_steps = pl.num_programs(0) * N_HEADS
    slot = step % 2
    scale = HEAD_DIM ** -0.5

    def page_copies(bb, hh, sl):
        copies = []
        for i in range(n_valid):
            for s in range(MOBA_TOPK):
                blk_idx = sel_ref[((bb * N_HEADS + hh) * n_valid + i) * MOBA_TOPK + s]
                for p in range(pages_per_block):
                    pg = pt_ref[bb, blk_idx * pages_per_block + p]
                    dst_rows = pl.ds(((i * MOBA_TOPK + s) * pages_per_block + p) * page, page)
                    copies.append(pltpu.make_async_copy(ck_hbm.at[layer, pg, :, hh, :],
                                                        kbuf.at[sl, dst_rows, :], sem.at[0, sl]))
                    copies.append(pltpu.make_async_copy(cv_hbm.at[layer, pg, :, hh, :],
                                                        vbuf.at[sl, dst_rows, :], sem.at[1, sl]))
        return copies

    @pl.when(step == 0)
    def _():
        for cp in page_copies(b, h, slot):
            cp.start()

    @pl.when(step + 1 < n_steps)
    def _():
        nxt = step + 1
        for cp in page_copies(nxt // N_HEADS, nxt % N_HEADS, 1 - slot):
            cp.start()

    for cp in page_copies(b, h, slot):
        cp.wait()

    qb = q_ref[...].astype(BF16)
    qrow = lax.broadcasted_iota(jnp.int32, (SAMPLE_ROWS, blk), 0)
    koff = lax.broadcasted_iota(jnp.int32, (SAMPLE_ROWS, blk), 1)
    orow = lax.broadcasted_iota(jnp.int32, (SAMPLE_ROWS, HEAD_DIM), 0)

    pad = jnp.zeros((HEAD_DIM - SAMPLE_ROWS, HEAD_DIM), F32)
    kn = jnp.concatenate([kn_ref[...], pad], axis=0).astype(BF16)
    vn = jnp.concatenate([vn_ref[...], pad], axis=0).astype(BF16)
    d_own = qrow[:, :HEAD_DIM] - koff[:, :HEAD_DIM]
    l_own = _dot_nt(qb, kn) * scale + _rel_bias_tile(tbl_ref, h, d_own)
    l_own = jnp.where(d_own >= 0, l_own, -jnp.inf)
    m_own = jnp.max(l_own, axis=1, keepdims=True)

    result = jnp.zeros((SAMPLE_ROWS, HEAD_DIM), F32)
    for i in range(n_valid):
        logits, values = [], []
        for s in range(MOBA_TOPK):
            sel = sel_ref[((b * N_HEADS + h) * n_valid + i) * MOBA_TOPK + s]
            rows = pl.ds((i * MOBA_TOPK + s) * blk, blk)
            dist = (past_len + qrow) - (sel * blk + koff)
            logits.append(_dot_nt(qb, kbuf[slot, rows, :].astype(BF16)) * scale + _rel_bias_tile(tbl_ref, h, dist))
            values.append(vbuf[slot, rows, :].astype(BF16))
        m = m_own
        for l_s in logits:
            m = jnp.maximum(m, jnp.max(l_s, axis=1, keepdims=True))
        p_own = jnp.exp(l_own - m)
        denom = jnp.sum(p_own, axis=1, keepdims=True)
        acc = _dot(p_own.astype(BF16), vn)
        for l_s, v_s in zip(logits, values):
            p_s = jnp.exp(l_s - m)
            denom = denom + jnp.sum(p_s, axis=1, keepdims=True)
            acc = acc + _dot(p_s.astype(BF16), v_s)
        result = jnp.where(orow == i, acc / denom, result)
    o_ref[...] = result.astype(o_ref.dtype)


def _moba_sample(page_table, sel_flat, tbl, proj, cache_k, cache_v, layer, n_seq, n_valid):
    page = cache_k.shape[2]
    past_len = page_table.shape[1] * page
    assert past_len % MOBA_BLOCK == 0 and MOBA_BLOCK % page == 0
    buf_rows = n_valid * MOBA_TOPK * MOBA_BLOCK

    def new_rows(off):
        return pl.BlockSpec((SAMPLE_ROWS, HEAD_DIM), lambda b, h, pt, sel: (b, off + h))

    return pl.pallas_call(
        functools.partial(_moba_sample_kernel, layer=layer, n_valid=n_valid, past_len=past_len, page=page),
        grid_spec=pltpu.PrefetchScalarGridSpec(
            num_scalar_prefetch=2,
            grid=(n_seq, N_HEADS),
            in_specs=[pl.BlockSpec(memory_space=pltpu.SMEM),
                      new_rows(QMB_BLK), new_rows(KMB_BLK), new_rows(VMB_BLK),
                      pl.BlockSpec(memory_space=pl.ANY), pl.BlockSpec(memory_space=pl.ANY)],
            out_specs=pl.BlockSpec((SAMPLE_ROWS, HEAD_DIM), lambda b, h, pt, sel: (b, h)),
            scratch_shapes=[pltpu.VMEM((2, buf_rows, HEAD_DIM), F32), pltpu.VMEM((2, buf_rows, HEAD_DIM), F32),
                            pltpu.SemaphoreType.DMA((2, 2))],
        ),
        out_shape=jax.ShapeDtypeStruct((n_seq * SAMPLE_ROWS, GROUP_W), BF16),
        compiler_params=_cparams("arbitrary", "arbitrary"),
        name="moba_sample",
    )(page_table, sel_flat, tbl, proj, proj, proj, cache_k, cache_v)


def _pad_row(v):
    return jnp.pad(v.astype(F32), (0, HEAD_DIM - v.shape[0])).reshape(1, HEAD_DIM)


def kernel(x_prompt, x_sample, cache_k, cache_v, page_table, state_conv, state_delta, w_in, conv_w, a_log, dt_bias,
           gn_w, w_out, norm_mix, norm_ffn, w_gate, w_up, w_down, rel_bias, final_norm):
    n_layers = w_in.shape[0]
    bp, t_len, d_model = x_prompt.shape
    bs, t_dec, _ = x_sample.shape
    conv_ch = 3 * GROUP_W
    a_off = conv_ch + GROUP_W
    mb_off = a_off + 2 * N_HEADS
    assert t_len % MOBA_BLOCK == 0 and CONV_W - 1 <= t_dec <= SAMPLE_ROWS
    assert cache_k.shape[3:] == (N_HEADS, HEAD_DIM)

    xp = x_prompt.reshape(bp * t_len, d_model)
    xs = jnp.pad(x_sample, ((0, 0), (0, SAMPLE_ROWS - t_dec), (0, 0))).reshape(bs * SAMPLE_ROWS, d_model)
    tm_p, tm_s = TM_PROMPT, bs * SAMPLE_ROWS

    kmean_t = _kmean_cache(cache_k, page_table).transpose(0, 1, 3, 2, 4)
    tbl = rel_bias.astype(F32).T

    conv_prev_p = jnp.zeros((1, bp, SUBLANES, conv_ch), F32)
    conv_prev_s = jnp.pad(state_conv.astype(F32), ((0, 0), (0, 0), (SUBLANES - (CONV_W - 1), 0), (0, 0)))
    s0_p = jnp.zeros((1, bp, N_HEADS, HEAD_DIM, HEAD_DIM), F32)
    s0_s = state_delta.astype(F32)

    outs = {name: [] for name in ("kp", "vp", "cp", "sp", "kd", "vd", "cd", "sd")}
    for l in range(n_layers):
        w_main = jnp.concatenate([w_in[l][:, :a_off], w_in[l][:, mb_off:]], axis=1).astype(BF16)
        w_ab = jnp.pad(w_in[l][:, a_off:mb_off], ((0, 0), (0, HEAD_DIM - 2 * N_HEADS))).astype(BF16)
        w_out_dn = w_out[l][:GROUP_W].astype(BF16)
        w_out_mb = w_out[l][GROUP_W:].astype(BF16)
        wg, wu, wd = w_gate[l].astype(BF16), w_up[l].astype(BF16), w_down[l].astype(BF16)
        nm, nf = norm_mix[l].reshape(1, d_model), norm_ffn[l].reshape(1, d_model)
        alog_row, dtb_row, gn_row = _pad_row(a_log[l]), _pad_row(dt_bias[l]), gn_w[l].reshape(1, HEAD_DIM)

        proj, proj_ab = _inproj(xp, nm, w_main, w_ab, tm_p)
        mix_dn, s_new = _deltanet(proj, proj_ab, conv_prev_p, conv_w[l], alog_row, dtb_row, gn_row, s0_p, l,
                                  bp, t_len, t_len)
        mix_mb = _moba_prompt(tbl, proj, bp, t_len)
        xp = _outproj(xp, mix_dn, mix_mb, w_out_dn, w_out_mb, tm_p)
        xp = _ffn_down(xp, _ffn_up(xp, nf, wg, wu, tm_p), wd, tm_p)
        proj3 = proj.reshape(bp, t_len, MAIN_COLS)
        outs["kp"].append(proj3[:, :, KMB_BLK * HEAD_DIM:VMB_BLK * HEAD_DIM].reshape(bp, t_len, N_HEADS, HEAD_DIM))
        outs["vp"].append(proj3[:, :, VMB_BLK * HEAD_DIM:].reshape(bp, t_len, N_HEADS, HEAD_DIM))
        outs["cp"].append(proj3[:, t_len - (CONV_W - 1):, :conv_ch])
        outs["sp"].append(s_new)

        proj, proj_ab = _inproj(xs, nm, w_main, w_ab, tm_s)
        mix_dn, s_new = _deltanet(proj, proj_ab, conv_prev_s, conv_w[l], alog_row, dtb_row, gn_row, s0_s, l,
                                  bs, SAMPLE_ROWS, t_dec)
        sel = _sample_select(proj, kmean_t, l, bs)
        sel_flat = sel[:, :, :t_dec, :MOBA_TOPK].reshape(-1)
        mix_mb = _moba_sample(page_table, sel_flat, tbl, proj, cache_k, cache_v, l, bs, t_dec)
        xs = _outproj(xs, mix_dn, mix_mb, w_out_dn, w_out_mb, tm_s)
        xs = _ffn_down(xs, _ffn_up(xs, nf, wg, wu, tm_s), wd, tm_s)
        proj3 = proj.reshape(bs, SAMPLE_ROWS, MAIN_COLS)
        outs["kd"].append(proj3[:, :t_dec, KMB_BLK * HEAD_DIM:VMB_BLK * HEAD_DIM].reshape(bs, t_dec, N_HEADS, HEAD_DIM))
        outs["vd"].append(proj3[:, :t_dec, VMB_BLK * HEAD_DIM:].reshape(bs, t_dec, N_HEADS, HEAD_DIM))
        outs["cd"].append(proj3[:, t_dec - (CONV_W - 1):t_dec, :conv_ch])
        outs["sd"].append(s_new)

    fn = final_norm.reshape(1, d_model)
    y_prompt = _final_norm(xp, fn, tm_p).reshape(bp, t_len, d_model)
    y_sample = _final_norm(xs, fn, tm_s).reshape(bs, SAMPLE_ROWS, d_model)[:, :t_dec]
    return (y_prompt, y_sample) + tuple(jnp.stack(outs[n]) for n in ("kp", "vp", "cp", "sp", "kd", "vd", "cd", "sd"))
```

```python
import functools
import math

import jax
import jax.numpy as jnp
import numpy as np
from jax import lax
from jax.experimental import pallas as pl
from jax.experimental.pallas import tpu as pltpu

F32 = jnp.float32
BF16 = jnp.bfloat16
HI = lax.Precision.HIGHEST

HEAD_DIM = 128
N_HEADS = 8
GROUP_W = N_HEADS * HEAD_DIM
CONV_W = 4
DN_CHUNK = 128
DN_HEADS_PER_STEP = 2
DN_CHAINS = 4
MOBA_BLOCK = 256
MOBA_TOPK = 3
REL_BUCKETS = 32
REL_MAX_DIST = 128
RMS_EPS = 1e-6
L2_EPS = 1e-6
SUBLANES = 8
SAMPLE_ROWS = 8

QDN_BLK, KDN_BLK, VDN_BLK, Z_BLK, QMB_BLK, KMB_BLK, VMB_BLK = 0, 8, 16, 24, 32, 40, 48
MAIN_COLS = 56 * HEAD_DIM

TM_PROMPT = 1024
TN_INPROJ = 1024
TN_OUTPROJ = 1024
TN_FFN_UP = 512
TN_FFN_DOWN = 512
VMEM_LIMIT = 56 * 1024 * 1024


def _bucket_thresholds():
    d = np.arange(0, 4 * REL_MAX_DIST)
    max_exact = REL_BUCKETS // 2
    nf = np.maximum(d, 1).astype(np.float32)
    large = max_exact + (np.log(nf / np.float32(max_exact)) / np.float32(math.log(REL_MAX_DIST / max_exact))
                         * (REL_BUCKETS - max_exact)).astype(np.int32)
    bucket = np.where(d < max_exact, d, np.minimum(large, REL_BUCKETS - 1))
    return tuple(int(np.argmax(bucket >= b)) for b in range(REL_BUCKETS))


BUCKET_THR = _bucket_thresholds()


def _cparams(*sem):
    return pltpu.CompilerParams(dimension_semantics=sem, vmem_limit_bytes=VMEM_LIMIT)


def _rms_rows(x, w_row):
    return x * lax.rsqrt(jnp.mean(x * x, axis=-1, keepdims=True) + RMS_EPS) * w_row


def _dot(a, b, precision=None):
    return jnp.dot(a, b, preferred_element_type=F32, precision=precision)


def _dot_nt(a, b, precision=None):
    return lax.dot_general(a, b, (((1,), (1,)), ((), ())), preferred_element_type=F32, precision=precision)


def _split_bf16(a):
    hi = a.astype(BF16)
    return hi, (a - hi.astype(F32)).astype(BF16)


def _dot3(a_parts, b_parts):
    (ah, al), (bh, bl) = a_parts, b_parts
    return _dot(ah, bh) + (_dot(ah, bl) + _dot(al, bh))


def _inproj_kernel(x_ref, nw_ref, w_ref, wab_ref, o_ref, oab_ref, h_scr):
    @pl.when(pl.program_id(1) == 0)
    def _():
        hb = _rms_rows(x_ref[...], nw_ref[...]).astype(BF16)
        h_scr[...] = hb
        oab_ref[...] = _dot(hb, wab_ref[...])

    o_ref[...] = _dot(h_scr[...], w_ref[...])


def _inproj(x, nw, w_main, w_ab, tm):
    m, d = x.shape
    n = w_main.shape[1]
    tn = TN_INPROJ
    return pl.pallas_call(
        _inproj_kernel,
        grid=(m // tm, n // tn),
        in_specs=[pl.BlockSpec((tm, d), lambda i, j: (i, 0)),
                  pl.BlockSpec((1, d), lambda i, j: (0, 0)),
                  pl.BlockSpec((d, tn), lambda i, j: (0, j)),
                  pl.BlockSpec((d, HEAD_DIM), lambda i, j: (0, 0))],
        out_specs=[pl.BlockSpec((tm, tn), lambda i, j: (i, j)),
                   pl.BlockSpec((tm, HEAD_DIM), lambda i, j: (i, 0))],
        out_shape=[jax.ShapeDtypeStruct((m, n), F32), jax.ShapeDtypeStruct((m, HEAD_DIM), F32)],
        scratch_shapes=[pltpu.VMEM((tm, d), BF16)],
        compiler_params=_cparams("parallel", "arbitrary"),
        name="inproj",
    )(x, nw, w_main, w_ab)


def _outproj_kernel(x_ref, a_ref, b_ref, wa_ref, wb_ref, o_ref):
    o_ref[...] = x_ref[...] + (_dot(a_ref[...], wa_ref[...]) + _dot(b_ref[...], wb_ref[...]))


def _outproj(x, mix_dn, mix_mb, w_dn, w_mb, tm):
    m, d = x.shape
    k = mix_dn.shape[1]
    tn = TN_OUTPROJ
    return pl.pallas_call(
        _outproj_kernel,
        grid=(m // tm, d // tn),
        in_specs=[pl.BlockSpec((tm, tn), lambda i, j: (i, j)),
                  pl.BlockSpec((tm, k), lambda i, j: (i, 0)),
                  pl.BlockSpec((tm, k), lambda i, j: (i, 0)),
                  pl.BlockSpec((k, tn), lambda i, j: (0, j)),
                  pl.BlockSpec((k, tn), lambda i, j: (0, j))],
        out_specs=pl.BlockSpec((tm, tn), lambda i, j: (i, j)),
        out_shape=jax.ShapeDtypeStruct((m, d), F32),
        compiler_params=_cparams("parallel", "arbitrary"),
        name="outproj",
    )(x, mix_dn, mix_mb, w_dn, w_mb)


def _ffn_up_kernel(x_ref, nw_ref, wg_ref, wu_ref, o_ref, h_scr):
    @pl.when(pl.program_id(1) == 0)
    def _():
        h_scr[...] = _rms_rows(x_ref[...], nw_ref[...]).astype(BF16)

    h = h_scr[...]
    g = _dot(h, wg_ref[...])
    u = _dot(h, wu_ref[...])
    o_ref[...] = (g * jax.nn.sigmoid(g) * u).astype(BF16)


def _ffn_up(x, nw, wg, wu, tm):
    m, d = x.shape
    n = wg.shape[1]
    tn = TN_FFN_UP
    return pl.pallas_call(
        _ffn_up_kernel,
        grid=(m // tm, n // tn),
        in_specs=[pl.BlockSpec((tm, d), lambda i, j: (i, 0)),
                  pl.BlockSpec((1, d), lambda i, j: (0, 0)),
                  pl.BlockSpec((d, tn), lambda i, j: (0, j)),
                  pl.BlockSpec((d, tn), lambda i, j: (0, j))],
        out_specs=pl.BlockSpec((tm, tn), lambda i, j: (i, j)),
        out_shape=jax.ShapeDtypeStruct((m, n), BF16),
        scratch_shapes=[pltpu.VMEM((tm, d), BF16)],
        compiler_params=_cparams("parallel", "arbitrary"),
        name="ffn_up",
    )(x, nw, wg, wu)


def _ffn_down_kernel(x_ref, a_ref, w_ref, o_ref):
    o_ref[...] = x_ref[...] + _dot(a_ref[...], w_ref[...])


def _ffn_down(x, act, wd, tm):
    m, d = x.shape
    k = act.shape[1]
    tn = TN_FFN_DOWN
    return pl.pallas_call(
        _ffn_down_kernel,
        grid=(m // tm, d // tn),
        in_specs=[pl.BlockSpec((tm, tn), lambda i, j: (i, j)),
                  pl.BlockSpec((tm, k), lambda i, j: (i, 0)),
                  pl.BlockSpec((k, tn), lambda i, j: (0, j))],
        out_specs=pl.BlockSpec((tm, tn), lambda i, j: (i, j)),
        out_shape=jax.ShapeDtypeStruct((m, d), F32),
        compiler_params=_cparams("parallel", "arbitrary"),
        name="ffn_down",
    )(x, act, wd)


def _final_norm_kernel(x_ref, nw_ref, o_ref):
    o_ref[...] = _rms_rows(x_ref[...], nw_ref[...])


def _final_norm(x, nw, tm):
    m, d = x.shape
    return pl.pallas_call(
        _final_norm_kernel,
        grid=(m // tm,),
        in_specs=[pl.BlockSpec((tm, d), lambda i: (i, 0)), pl.BlockSpec((1, d), lambda i: (0, 0))],
        out_specs=pl.BlockSpec((tm, d), lambda i: (i, 0)),
        out_shape=jax.ShapeDtypeStruct((m, d), F32),
        compiler_params=_cparams("parallel"),
        name="final_norm",
    )(x, nw)


def _deltanet_kernel(q_ref, k_ref, v_ref, z_ref, ab_ref, pq_ref, pk_ref, pv_ref, cq_ref, ck_ref, cv_ref,
                     alog_ref, dtb_ref, gn_ref, s0_ref, o_ref, s_ref,
                     u_scr, wq_scr, kdt_scr, attn_scr, egl_scr, *, t_rows, t_valid):
    c = DN_CHUNK
    hd = HEAD_DIM
    heads = range(DN_HEADS_PER_STEP)
    head0 = pl.program_id(1) * DN_HEADS_PER_STEP
    n_chunks = max(t_rows // c, 1)
    rows = min(t_rows, c)
    n_doublings = max((min(c, t_valid) - 1).bit_length() - 1, 0)
    group = math.gcd(n_chunks, max(DN_CHAINS // DN_HEADS_PER_STEP, 1))

    ii = lax.broadcasted_iota(jnp.int32, (c, c), 0)
    jj = lax.broadcasted_iota(jnp.int32, (c, c), 1)
    tril = ii >= jj
    strict = ii > jj
    tril_f = tril.astype(F32)
    eye_f = (ii == jj).astype(F32)
    ones_cc = jnp.ones((c, c), F32)
    lane = lax.broadcasted_iota(jnp.int32, (c, hd), 1)
    row = lax.broadcasted_iota(jnp.int32, (c, hd), 0)
    neg_decay_rate = -jnp.exp(alog_ref[...])
    dtb = dtb_ref[...]
    gn = gn_ref[...]

    def each(fn, *lists):
        return [fn(*args) for args in zip(*lists)]

    def cols(hh):
        return slice(hh * hd, (hh + 1) * hd)

    def load(ref, r0, hh):
        x = ref[pl.ds(r0, rows), cols(hh)]
        if rows < c:
            x = jnp.concatenate([x, jnp.zeros((c - rows, hd), F32)], axis=0)
        return x

    def conv_silu(ref, prev_ref, taps_ref, ci, hh):
        r0 = pl.multiple_of(ci * rows, rows)
        x = load(ref, r0, hh)
        tail = prev_ref[:, cols(hh)]
        if n_chunks > 1:
            before = ref[pl.ds(pl.multiple_of(jnp.maximum(r0 - SUBLANES, 0), SUBLANES), SUBLANES), cols(hh)]
            tail = jnp.where(ci == 0, tail, before)
        win = jnp.concatenate([tail, x], axis=0)
        taps = taps_ref[:, cols(hh)]
        y = win[5:5 + c] * taps[0:1]
        for i in range(1, CONV_W):
            y = y + win[5 + i:5 + i + c] * taps[i:i + 1]
        return y * jax.nn.sigmoid(y)

    def l2n(x):
        return x * lax.rsqrt(jnp.sum(x * x, axis=-1, keepdims=True) + L2_EPS)

    def gates(ci, hh):
        ab = ab_ref[pl.ds(pl.multiple_of(ci * rows, rows), rows), :]
        if rows < c:
            ab = jnp.concatenate([ab, jnp.zeros((c - rows, hd), F32)], axis=0)
        z_ab = ab + dtb
        softplus = jnp.maximum(z_ab, 0.0) + jnp.log1p(jnp.exp(-jnp.abs(z_ab)))
        g_sel, b_sel = lane == head0 + hh, lane == N_HEADS + head0 + hh
        if t_valid < n_chunks * c:
            valid = (row + ci * c) < t_valid
            g_sel, b_sel = g_sel & valid, b_sel & valid
        g_all = jnp.where(g_sel, neg_decay_rate * softplus, 0.0)
        beta = jnp.sum(jnp.where(b_sel, jax.nn.sigmoid(ab), 0.0), axis=1, keepdims=True)
        return g_all, beta

    def prepare(gi, carry):
        cis = [gi * group + t for t in range(group) for _ in heads]
        hhs = [hh for _ in range(group) for hh in heads]
        q = each(lambda ci, hh: l2n(conv_silu(q_ref, pq_ref, cq_ref, ci, hh)) * (hd ** -0.5), cis, hhs)
        k = each(lambda ci, hh: l2n(conv_silu(k_ref, pk_ref, ck_ref, ci, hh)), cis, hhs)
        v = each(lambda ci, hh: conv_silu(v_ref, pv_ref, cv_ref, ci, hh), cis, hhs)
        g_all, beta = zip(*each(gates, cis, hhs))
        gc = each(lambda g: jnp.sum(_dot(tril_f, g, HI), axis=1, keepdims=True), g_all)
        gc_row = each(lambda g: _dot(ones_cc, eye_f * g, HI), gc)
        decay = each(lambda g, gr: jnp.where(tril, jnp.exp(jnp.where(tril, g - gr, 0.0)), 0.0), gc, gc_row)
        e_gc = each(jnp.exp, gc)
        g_last = each(lambda g: g[c - 1:c, :], gc)
        kb = each(lambda a, bt: a * bt, k, beta)
        vb = each(lambda a, bt: a * bt, v, beta)
        k16 = each(lambda a: a.astype(BF16), k)
        neg_l = each(lambda a, b16, dc: jnp.where(strict, -(_dot_nt(a.astype(BF16), b16) * dc), 0.0), kb, k16, decay)
        tinv = each(lambda m: eye_f + m, neg_l)
        p = each(_split_bf16, neg_l)
        for _ in range(n_doublings):
            p = each(lambda pp: _split_bf16(_dot3(pp, pp)), p)
            tinv = each(lambda t, pp: t + _dot3(_split_bf16(t), pp), tinv, p)
        uw = each(lambda t, a, b, e: _dot3(_split_bf16(t), _split_bf16(jnp.concatenate([a, b * e], axis=1))),
                  tinv, vb, kb, e_gc)
        attn = each(lambda a, b16, dc: jnp.where(tril, _dot_nt(a.astype(BF16), b16) * dc, 0.0).astype(BF16),
                    q, k16, decay)
        kdt = each(lambda a, gl, g: (a * jnp.exp(gl - g)).T.astype(BF16), k, g_last, gc)
        for n, (ci, hh) in enumerate(zip(cis, hhs)):
            c0 = pl.multiple_of(ci * c, c)
            u_scr[hh, pl.ds(c0, c), :] = uw[n][:, :hd]
            wq_scr[hh, pl.ds(2 * c0, c), :] = uw[n][:, hd:].astype(BF16)
            wq_scr[hh, pl.ds(2 * c0 + c, c), :] = (q[n] * e_gc[n]).astype(BF16)
            attn_scr[hh, pl.ds(c0, c), :] = attn[n]
            kdt_scr[hh, pl.ds(c0, c), :] = kdt[n]
            egl_scr[hh, pl.ds(pl.multiple_of(ci * SUBLANES, SUBLANES), SUBLANES), :] = jnp.broadcast_to(
                jnp.exp(g_last[n]), (SUBLANES, hd))
        return carry

    lax.fori_loop(0, n_chunks // group, prepare, 0)

    def recur(ci, states):
        r0 = pl.multiple_of(ci * rows, rows)
        c0 = pl.multiple_of(ci * c, c)
        e0 = pl.multiple_of(ci * SUBLANES, SUBLANES)
        r = each(lambda hh, s: _dot(wq_scr[hh, pl.ds(2 * c0, 2 * c), :], s.astype(BF16)), heads, states)
        v_new = each(lambda hh, rr: (u_scr[hh, pl.ds(c0, c), :] - rr[:c]).astype(BF16), heads, r)
        o = each(lambda hh, rr, vn: rr[c:] + _dot(attn_scr[hh, pl.ds(c0, c), :], vn), heads, r, v_new)
        new_states = each(lambda hh, s, vn: s * egl_scr[hh, pl.ds(e0, SUBLANES), :][0:1]
                          + _dot(kdt_scr[hh, pl.ds(c0, c), :], vn), heads, states, v_new)
        for hh in heads:
            on = o[hh] * lax.rsqrt(jnp.mean(o[hh] * o[hh], axis=-1, keepdims=True) + RMS_EPS) * gn
            zc = load(z_ref, r0, hh)
            out = (on * (zc * jax.nn.sigmoid(zc))).astype(o_ref.dtype)
            o_ref[pl.ds(r0, rows), cols(hh)] = out[:rows]
        return tuple(new_states)

    final = lax.fori_loop(0, n_chunks, recur, tuple(s0_ref[hh] for hh in heads))
    for hh in heads:
        s_ref[hh] = final[hh]


def _deltanet(proj, proj_ab, conv_prev, conv_w, alog_row, dtb_row, gn_row, s0, layer, n_seq, t_rows, t_valid):
    lp = layer if conv_prev.shape[0] > 1 else 0
    t_pad = max(t_rows // DN_CHUNK, 1) * DN_CHUNK
    n_chunks = t_pad // DN_CHUNK
    hps = DN_HEADS_PER_STEP
    width = hps * HEAD_DIM
    assert N_HEADS % hps == 0

    def col(off):
        return pl.BlockSpec((t_rows, width), lambda b, h: (b, off // hps + h))

    def prev(off):
        return pl.BlockSpec((None, None, SUBLANES, width), lambda b, h: (lp, b, 0, off // hps + h))

    def taps(off):
        return pl.BlockSpec((CONV_W, width), lambda b, h: (0, off // hps + h))

    row_spec = pl.BlockSpec((1, HEAD_DIM), lambda b, h: (0, 0))
    return pl.pallas_call(
        functools.partial(_deltanet_kernel, t_rows=t_rows, t_valid=t_valid),
        grid=(n_seq, N_HEADS // hps),
        in_specs=[col(QDN_BLK), col(KDN_BLK), col(VDN_BLK), col(Z_BLK),
                  pl.BlockSpec((t_rows, HEAD_DIM), lambda b, h: (b, 0)),
                  prev(QDN_BLK), prev(KDN_BLK), prev(VDN_BLK),
                  taps(QDN_BLK), taps(KDN_BLK), taps(VDN_BLK),
                  row_spec, row_spec, row_spec,
                  pl.BlockSpec((None, None, hps, HEAD_DIM, HEAD_DIM), lambda b, h: (lp, b, h, 0, 0))],
        out_specs=[pl.BlockSpec((t_rows, width), lambda b, h: (b, h)),
                   pl.BlockSpec((None, hps, HEAD_DIM, HEAD_DIM), lambda b, h: (b, h, 0, 0))],
        out_shape=[jax.ShapeDtypeStruct((n_seq * t_rows, GROUP_W), BF16),
                   jax.ShapeDtypeStruct((n_seq, N_HEADS, HEAD_DIM, HEAD_DIM), F32)],
        scratch_shapes=[pltpu.VMEM((hps, t_pad, HEAD_DIM), F32),
                        pltpu.VMEM((hps, 2 * t_pad, HEAD_DIM), BF16),
                        pltpu.VMEM((hps, t_pad, DN_CHUNK), BF16),
                        pltpu.VMEM((hps, t_pad, DN_CHUNK), BF16),
                        pltpu.VMEM((hps, n_chunks * SUBLANES, HEAD_DIM), F32)],
        compiler_params=_cparams("parallel", "arbitrary"),
        name="deltanet",
    )(proj, proj, proj, proj, proj_ab, conv_prev, conv_prev, conv_prev, conv_w, conv_w, conv_w,
      alog_row, dtb_row, gn_row, s0)


def _rel_bias_tile(tbl_ref, h, dist):
    val = jnp.full(dist.shape, tbl_ref[h, 0], F32)
    for b in range(1, REL_BUCKETS):
        val = jnp.where(dist >= BUCKET_THR[b], tbl_ref[h, b], val)
    return val


def _gate_rank(gate, n_cand, idx, axis):
    rank = jnp.zeros(gate.shape, F32)
    for m in range(n_cand):
        gm = lax.slice_in_dim(gate, m, m + 1, axis=axis)
        ahead = (gm > gate) | ((gm == gate) & (m < idx))
        rank = rank + ahead.astype(F32)
    return rank


def _moba_prompt_kernel(tbl_ref, q_ref, k_ref, v_ref, o_ref, kmean_scr, bown_scr, bnear_scr, k16_scr, vt16_scr,
                        *, n_blocks):
    blk = MOBA_BLOCK
    h = pl.program_id(0)
    b = pl.program_id(1)
    j = pl.program_id(2)
    scale = HEAD_DIM ** -0.5

    @pl.when((b == 0) & (j == 0))
    def _():
        ki = lax.broadcasted_iota(jnp.int32, (blk, blk), 0)
        qi = lax.broadcasted_iota(jnp.int32, (blk, blk), 1)
        d = qi - ki
        bown_scr[...] = jnp.where(d >= 0, _rel_bias_tile(tbl_ref, h, d), -jnp.inf)
        bnear_scr[...] = _rel_bias_tile(tbl_ref, h, d + blk)

    @pl.when(j == 0)
    def _():
        kmean_scr[...] = jnp.zeros(kmean_scr.shape, F32)
        for n in range(n_blocks):
            kmean_scr[n:n + 1, :] = jnp.sum(k_ref[n * blk:(n + 1) * blk, :], axis=0, keepdims=True) * (1.0 / blk)
        k16_scr[...] = k_ref[...].astype(BF16)
        for n in range(n_blocks * blk // HEAD_DIM):
            rows = slice(n * HEAD_DIM, (n + 1) * HEAD_DIM)
            vt16_scr[:, rows] = v_ref[rows, :].T.astype(BF16)

    far_bias = tbl_ref[h, REL_BUCKETS - 1]

    def attend(jb):
        q = q_ref[...]
        q16 = q.astype(BF16)
        n_keys = (jb + 1) * blk
        gated = jb > MOBA_TOPK
        if gated:
            cand = lax.broadcasted_iota(jnp.int32, (SUBLANES, blk), 0)
            gate_t = jnp.where(cand < jb, _dot_nt(kmean_scr[...], q, HI)[:SUBLANES], -jnp.inf)
            keep_t = (_gate_rank(gate_t, jb, cand, 0) < MOBA_TOPK).astype(F32)
        pieces = []
        for n in range(jb + 1):
            sn = _dot_nt(k16_scr[n * blk:(n + 1) * blk, :], q16) * scale
            if n == jb:
                sn = sn + bown_scr[...]
            else:
                sn = sn + (bnear_scr[...] if n == jb - 1 else far_bias)
                if gated:
                    sn = jnp.where(keep_t[n:n + 1, :] > 0.0, sn, -jnp.inf)
            pieces.append(sn)
        m = jnp.max(pieces[0], axis=0, keepdims=True)
        for sn in pieces[1:]:
            m = jnp.maximum(m, jnp.max(sn, axis=0, keepdims=True))
        probs = [jnp.exp(sn - m) for sn in pieces]
        denom = jnp.sum(probs[0], axis=0, keepdims=True)
        for pn in probs[1:]:
            denom = denom + jnp.sum(pn, axis=0, keepdims=True)
        p = jnp.concatenate([pn.astype(BF16) for pn in probs], axis=0) if jb else probs[0].astype(BF16)
        o_t = _dot(vt16_scr[:, 0:n_keys], p) / denom
        o_ref[...] = o_t.T.astype(o_ref.dtype)

    for jb in range(n_blocks):
        pl.when(j == jb)(functools.partial(attend, jb))


def _moba_prompt(tbl, proj, n_seq, t_len):
    blk = MOBA_BLOCK
    n_blocks = t_len // blk
    assert MOBA_BLOCK >= BUCKET_THR[-1] and n_blocks <= SUBLANES
    return pl.pallas_call(
        functools.partial(_moba_prompt_kernel, n_blocks=n_blocks),
        grid=(N_HEADS, n_seq, n_blocks),
        in_specs=[pl.BlockSpec(memory_space=pltpu.SMEM),
                  pl.BlockSpec((blk, HEAD_DIM), lambda h, b, j: (b * n_blocks + j, QMB_BLK + h)),
                  pl.BlockSpec((t_len, HEAD_DIM), lambda h, b, j: (b, KMB_BLK + h)),
                  pl.BlockSpec((t_len, HEAD_DIM), lambda h, b, j: (b, VMB_BLK + h))],
        out_specs=pl.BlockSpec((blk, HEAD_DIM), lambda h, b, j: (b * n_blocks + j, h)),
        out_shape=jax.ShapeDtypeStruct((n_seq * t_len, GROUP_W), BF16),
        scratch_shapes=[pltpu.VMEM((HEAD_DIM, HEAD_DIM), F32),
                        pltpu.VMEM((blk, blk), F32), pltpu.VMEM((blk, blk), F32),
                        pltpu.VMEM((t_len, HEAD_DIM), BF16), pltpu.VMEM((HEAD_DIM, t_len), BF16)],
        compiler_params=_cparams("arbitrary", "arbitrary", "arbitrary"),
        name="moba_prompt",
    )(tbl, proj, proj, proj)


PAGES_PER_STEP = 16


def _kmean_cache_kernel(pt_ref, *refs):
    del pt_ref
    pages, o_ref = refs[:PAGES_PER_STEP], refs[PAGES_PER_STEP]
    pages_per_block = PAGES_PER_STEP // SUBLANES
    for n in range(SUBLANES):
        tot = jnp.sum(pages[n * pages_per_block][...], axis=0)
        for p in range(1, pages_per_block):
            tot = tot + jnp.sum(pages[n * pages_per_block + p][...], axis=0)
        o_ref[n] = tot * (1.0 / MOBA_BLOCK)


def _kmean_cache(cache_k, page_table):
    n_layers, _, page, n_heads, hd = cache_k.shape
    n_seq, n_pages = page_table.shape
    assert MOBA_BLOCK % page == 0 and PAGES_PER_STEP * page == SUBLANES * MOBA_BLOCK
    n_past_blocks = n_pages * page // MOBA_BLOCK

    def page_spec(p):
        return pl.BlockSpec((None, None, page, n_heads, hd),
                            lambda l, b, n, pt: (l, pt[b, n * PAGES_PER_STEP + p], 0, 0, 0))

    return pl.pallas_call(
        _kmean_cache_kernel,
        grid_spec=pltpu.PrefetchScalarGridSpec(
            num_scalar_prefetch=1,
            grid=(n_layers, n_seq, n_pages // PAGES_PER_STEP),
            in_specs=[page_spec(p) for p in range(PAGES_PER_STEP)],
            out_specs=pl.BlockSpec((None, None, SUBLANES, n_heads, hd), lambda l, b, n, pt: (l, b, n, 0, 0)),
        ),
        out_shape=jax.ShapeDtypeStruct((n_layers, n_seq, n_past_blocks, n_heads, hd), F32),
        compiler_params=_cparams("arbitrary", "arbitrary", "arbitrary"),
        name="kmean_cache",
    )(page_table, *([cache_k] * PAGES_PER_STEP))


def _sample_select_kernel(q_ref, km_ref, o_ref, *, n_past_blocks):
    q = q_ref[...]
    km = km_ref[...]
    km = jnp.concatenate([km, jnp.zeros((HEAD_DIM - n_past_blocks, HEAD_DIM), F32)], axis=0)
    lane = lax.broadcasted_iota(jnp.int32, (SAMPLE_ROWS, HEAD_DIM), 1)
    gate = jnp.where(lane < n_past_blocks, _dot_nt(q, km, HI), -jnp.inf)
    rank = _gate_rank(gate, n_past_blocks, lane, 1)
    lane_f = lane.astype(F32)
    out = jnp.zeros((SAMPLE_ROWS, HEAD_DIM), F32)
    for r in range(MOBA_TOPK):
        idx = jnp.sum(jnp.where(rank == r, lane_f, 0.0), axis=1, keepdims=True)
        out = jnp.where(lane == r, idx, out)
    o_ref[...] = out.astype(jnp.int32)


def _sample_select(proj, kmean_t, layer, n_seq):
    n_past_blocks = kmean_t.shape[3]
    assert MOBA_TOPK <= n_past_blocks <= HEAD_DIM
    return pl.pallas_call(
        functools.partial(_sample_select_kernel, n_past_blocks=n_past_blocks),
        grid=(n_seq, N_HEADS),
        in_specs=[pl.BlockSpec((SAMPLE_ROWS, HEAD_DIM), lambda b, h: (b, QMB_BLK + h)),
                  pl.BlockSpec((None, None, None, n_past_blocks, HEAD_DIM), lambda b, h: (layer, b, h, 0, 0))],
        out_specs=pl.BlockSpec((None, None, SAMPLE_ROWS, HEAD_DIM), lambda b, h: (b, h, 0, 0)),
        out_shape=jax.ShapeDtypeStruct((n_seq, N_HEADS, SAMPLE_ROWS, HEAD_DIM), jnp.int32),
        compiler_params=_cparams("parallel", "arbitrary"),
        name="sample_select",
    )(proj, kmean_t)


def _moba_sample_kernel(pt_ref, sel_ref, tbl_ref, q_ref, kn_ref, vn_ref, ck_hbm, cv_hbm, o_ref, kbuf, vbuf, sem,
                        *, layer, n_valid, past_len, page):
    blk = MOBA_BLOCK
    pages_per_block = blk // page
    b, h = pl.program_id(0), pl.program_id(1)
    step = b * N_HEADS + h
    n_steps = pl.num_programs(0) * N_HEADS
    slot = step % 2
    scale = HEAD_DIM ** -0.5

    def page_copies(bb, hh, sl):
        copies = []
        for i in range(n_valid):
            for s in range(MOBA_TOPK):
                blk_idx = sel_ref[((bb * N_HEADS + hh) * n_valid + i) * MOBA_TOPK + s]
                for p in range(pages_per_block):
                    pg = pt_ref[bb, blk_idx * pages_per_block + p]
                    dst_rows = pl.ds(((i * MOBA_TOPK + s) * pages_per_block + p) * page, page)
                    copies.append(pltpu.make_async_copy(ck_hbm.at[layer, pg, :, hh, :],
                                                        kbuf.at[sl, dst_rows, :], sem.at[0, sl]))
                    copies.append(pltpu.make_async_copy(cv_hbm.at[layer, pg, :, hh, :],
                                                        vbuf.at[sl, dst_rows, :], sem.at[1, sl]))
        return copies

    @pl.when(step == 0)
    def _():
        for cp in page_copies(b, h, slot):
            cp.start()

    @pl.when(step + 1 < n_steps)
    def _():
        nxt = step + 1
        for cp in page_copies(nxt // N_HEADS, nxt % N_HEADS, 1 - slot):
            cp.start()

    for cp in page_copies(b, h, slot):
        cp.wait()

    qb = q_ref[...].astype(BF16)
    qrow = lax.broadcasted_iota(jnp.int32, (SAMPLE_ROWS, blk), 0)
    koff = lax.broadcasted_iota(jnp.int32, (SAMPLE_ROWS, blk), 1)
    orow = lax.broadcasted_iota(jnp.int32, (SAMPLE_ROWS, HEAD_DIM), 0)

    pad = jnp.zeros((HEAD_DIM - SAMPLE_ROWS, HEAD_DIM), F32)
    kn = jnp.concatenate([kn_ref[...], pad], axis=0).astype(BF16)
    vn = jnp.concatenate([vn_ref[...], pad], axis=0).astype(BF16)
    d_own = qrow[:, :HEAD_DIM] - koff[:, :HEAD_DIM]
    l_own = _dot_nt(qb, kn) * scale + _rel_bias_tile(tbl_ref, h, d_own)
    l_own = jnp.where(d_own >= 0, l_own, -jnp.inf)
    m_own = jnp.max(l_own, axis=1, keepdims=True)

    result = jnp.zeros((SAMPLE_ROWS, HEAD_DIM), F32)
    for i in range(n_valid):
        logits, values = [], []
        for s in range(MOBA_TOPK):
            sel = sel_ref[((b * N_HEADS + h) * n_valid + i) * MOBA_TOPK + s]
            rows = pl.ds((i * MOBA_TOPK + s) * blk, blk)
            dist = (past_len + qrow) - (sel * blk + koff)
            logits.append(_dot_nt(qb, kbuf[slot, rows, :].astype(BF16)) * scale + _rel_bias_tile(tbl_ref, h, dist))
            values.append(vbuf[slot, rows, :].astype(BF16))
        m = m_own
        for l_s in logits:
            m = jnp.maximum(m, jnp.max(l_s, axis=1, keepdims=True))
        p_own = jnp.exp(l_own - m)
        denom = jnp.sum(p_own, axis=1, keepdims=True)
        acc = _dot(p_own.astype(BF16), vn)
        for l_s, v_s in zip(logits, values):
            p_s = jnp.exp(l_s - m)
            denom = denom + jnp.sum(p_s, axis=1, keepdims=True)
            acc = acc + _dot(p_s.astype(BF16), v_s)
        result = jnp.where(orow == i, acc / denom, result)
    o_ref[...] = result.astype(o_ref.dtype)


def _moba_sample(page_table, sel_flat, tbl, proj, cache_k, cache_v, layer, n_seq, n_valid):
    page = cache_k.shape[2]
    past_len = page_table.shape[1] * page
    assert past_len % MOBA_BLOCK == 0 and MOBA_BLOCK % page == 0
    buf_rows = n_valid * MOBA_TOPK * MOBA_BLOCK

    def new_rows(off):
        return pl.BlockSpec((SAMPLE_ROWS, HEAD_DIM), lambda b, h, pt, sel: (b, off + h))

    return pl.pallas_call(
        functools.partial(_moba_sample_kernel, layer=layer, n_valid=n_valid, past_len=past_len, page=page),
        grid_spec=pltpu.PrefetchScalarGridSpec(
            num_scalar_prefetch=2,
            grid=(n_seq, N_HEADS),
            in_specs=[pl.BlockSpec(memory_space=pltpu.SMEM),
                      new_rows(QMB_BLK), new_rows(KMB_BLK), new_rows(VMB_BLK),
                      pl.BlockSpec(memory_space=pl.ANY), pl.BlockSpec(memory_space=pl.ANY)],
            out_specs=pl.BlockSpec((SAMPLE_ROWS, HEAD_DIM), lambda b, h, pt, sel: (b, h)),
            scratch_shapes=[pltpu.VMEM((2, buf_rows, HEAD_DIM), F32), pltpu.VMEM((2, buf_rows, HEAD_DIM), F32),
                            pltpu.SemaphoreType.DMA((2, 2))],
        ),
        out_shape=jax.ShapeDtypeStruct((n_seq * SAMPLE_ROWS, GROUP_W), BF16),
        compiler_params=_cparams("arbitrary", "arbitrary"),
        name="moba_sample",
    )(page_table, sel_flat, tbl, proj, proj, proj, cache_k, cache_v)


def _pad_row(v):
    return jnp.pad(v.astype(F32), (0, HEAD_DIM - v.shape[0])).reshape(1, HEAD_DIM)


def kernel(x_prompt, x_sample, cache_k, cache_v, page_table, state_conv, state_delta, w_in, conv_w, a_log, dt_bias,
           gn_w, w_out, norm_mix, norm_ffn, w_gate, w_up, w_down, rel_bias, final_norm):
    n_layers = w_in.shape[0]
    bp, t_len, d_model = x_prompt.shape
    bs, t_dec, _ = x_sample.shape
    conv_ch = 3 * GROUP_W
    a_off = conv_ch + GROUP_W
    mb_off = a_off + 2 * N_HEADS
    assert t_len % MOBA_BLOCK == 0 and CONV_W - 1 <= t_dec <= SAMPLE_ROWS
    assert cache_k.shape[3:] == (N_HEADS, HEAD_DIM)

    xp = x_prompt.reshape(bp * t_len, d_model)
    xs = jnp.pad(x_sample, ((0, 0), (0, SAMPLE_ROWS - t_dec), (0, 0))).reshape(bs * SAMPLE_ROWS, d_model)
    tm_p, tm_s = TM_PROMPT, bs * SAMPLE_ROWS

    kmean_t = _kmean_cache(cache_k, page_table).transpose(0, 1, 3, 2, 4)
    tbl = rel_bias.astype(F32).T

    conv_prev_p = jnp.zeros((1, bp, SUBLANES, conv_ch), F32)
    conv_prev_s = jnp.pad(state_conv.astype(F32), ((0, 0), (0, 0), (SUBLANES - (CONV_W - 1), 0), (0, 0)))
    s0_p = jnp.zeros((1, bp, N_HEADS, HEAD_DIM, HEAD_DIM), F32)
    s0_s = state_delta.astype(F32)

    outs = {name: [] for name in ("kp", "vp", "cp", "sp", "kd", "vd", "cd", "sd")}
    for l in range(n_layers):
        w_main = jnp.concatenate([w_in[l][:, :a_off], w_in[l][:, mb_off:]], axis=1).astype(BF16)
        w_ab = jnp.pad(w_in[l][:, a_off:mb_off], ((0, 0), (0, HEAD_DIM - 2 * N_HEADS))).astype(BF16)
        w_out_dn = w_out[l][:GROUP_W].astype(BF16)
        w_out_mb = w_out[l][GROUP_W:].astype(BF16)
        wg, wu, wd = w_gate[l].astype(BF16), w_up[l].astype(BF16), w_down[l].astype(BF16)
        nm, nf = norm_mix[l].reshape(1, d_model), norm_ffn[l].reshape(1, d_model)
        alog_row, dtb_row, gn_row = _pad_row(a_log[l]), _pad_row(dt_bias[l]), gn_w[l].reshape(1, HEAD_DIM)

        proj, proj_ab = _inproj(xp, nm, w_main, w_ab, tm_p)
        mix_dn, s_new = _deltanet(proj, proj_ab, conv_prev_p, conv_w[l], alog_row, dtb_row, gn_row, s0_p, l,
                                  bp, t_len, t_len)
        mix_mb = _moba_prompt(tbl, proj, bp, t_len)
        xp = _outproj(xp, mix_dn, mix_mb, w_out_dn, w_out_mb, tm_p)
        xp = _ffn_down(xp, _ffn_up(xp, nf, wg, wu, tm_p), wd, tm_p)
        proj3 = proj.reshape(bp, t_len, MAIN_COLS)
        outs["kp"].append(proj3[:, :, KMB_BLK * HEAD_DIM:VMB_BLK * HEAD_DIM].reshape(bp, t_len, N_HEADS, HEAD_DIM))
        outs["vp"].append(proj3[:, :, VMB_BLK * HEAD_DIM:].reshape(bp, t_len, N_HEADS, HEAD_DIM))
        outs["cp"].append(proj3[:, t_len - (CONV_W - 1):, :conv_ch])
        outs["sp"].append(s_new)

        proj, proj_ab = _inproj(xs, nm, w_main, w_ab, tm_s)
        mix_dn, s_new = _deltanet(proj, proj_ab, conv_prev_s, conv_w[l], alog_row, dtb_row, gn_row, s0_s, l,
                                  bs, SAMPLE_ROWS, t_dec)
        sel = _sample_select(proj, kmean_t, l, bs)
        sel_flat = sel[:, :, :t_dec, :MOBA_TOPK].reshape(-1)
        mix_mb = _moba_sample(page_table, sel_flat, tbl, proj, cache_k, cache_v, l, bs, t_dec)
        xs = _outproj(xs, mix_dn, mix_mb, w_out_dn, w_out_mb, tm_s)
        xs = _ffn_down(xs, _ffn_up(xs, nf, wg, wu, tm_s), wd, tm_s)
        proj3 = proj.reshape(bs, SAMPLE_ROWS, MAIN_COLS)
        outs["kd"].append(proj3[:, :t_dec, KMB_BLK * HEAD_DIM:VMB_BLK * HEAD_DIM].reshape(bs, t_dec, N_HEADS, HEAD_DIM))
        outs["vd"].append(proj3[:, :t_dec, VMB_BLK * HEAD_DIM:].reshape(bs, t_dec, N_HEADS, HEAD_DIM))
        outs["cd"].append(proj3[:, t_dec - (CONV_W - 1):t_dec, :conv_ch])
        outs["sd"].append(s_new)

    fn = final_norm.reshape(1, d_model)
    y_prompt = _final_norm(xp, fn, tm_p).reshape(bp, t_len, d_model)
    y_sample = _final_norm(xs, fn, tm_s).reshape(bs, SAMPLE_ROWS, d_model)[:, :t_dec]
    return (y_prompt, y_sample) + tuple(jnp.stack(outs[n]) for n in ("kp", "vp", "cp", "sp", "kd", "vd", "cd", "sd"))
```

```python
import functools
import math

import jax
import jax.numpy as jnp
import numpy as np
from jax import lax
from jax.experimental import pallas as pl
from jax.experimental.pallas import tpu as pltpu

F32 = jnp.float32
BF16 = jnp.bfloat16
HI = lax.Precision.HIGHEST

HEAD_DIM = 128
N_HEADS = 8
GROUP_W = N_HEADS * HEAD_DIM
CONV_W = 4
DN_CHUNK = 128
DN_HEADS_PER_STEP = 2
DN_CHAINS = 4
MOBA_BLOCK = 256
MOBA_TOPK = 3
MOBA_HEADS_PER_STEP = 2
REL_BUCKETS = 32
REL_MAX_DIST = 128
RMS_EPS = 1e-6
L2_EPS = 1e-6
SUBLANES = 8
SAMPLE_ROWS = 8

QDN_BLK, KDN_BLK, VDN_BLK, Z_BLK, QMB_BLK, KMB_BLK, VMB_BLK = 0, 8, 16, 24, 32, 40, 48
MAIN_COLS = 56 * HEAD_DIM

TM_PROMPT = 1024
TN_INPROJ = 1024
TN_OUTPROJ = 1024
TN_FFN_UP = 512
TN_FFN_DOWN = 512
PAGES_PER_STEP = 16
VMEM_LIMIT = 56 * 1024 * 1024


def _bucket_thresholds():
    d = np.arange(0, 4 * REL_MAX_DIST)
    max_exact = REL_BUCKETS // 2
    nf = np.maximum(d, 1).astype(np.float32)
    large = max_exact + (np.log(nf / np.float32(max_exact)) / np.float32(math.log(REL_MAX_DIST / max_exact))
                         * (REL_BUCKETS - max_exact)).astype(np.int32)
    bucket = np.where(d < max_exact, d, np.minimum(large, REL_BUCKETS - 1))
    return tuple(int(np.argmax(bucket >= b)) for b in range(REL_BUCKETS))


BUCKET_THR = _bucket_thresholds()


def _cparams(*sem):
    return pltpu.CompilerParams(dimension_semantics=sem, vmem_limit_bytes=VMEM_LIMIT)


def _rms_rows(x, w_row):
    return x * lax.rsqrt(jnp.mean(x * x, axis=-1, keepdims=True) + RMS_EPS) * w_row


def _dot(a, b, precision=None):
    return jnp.dot(a, b, preferred_element_type=F32, precision=precision)


def _dot_nt(a, b, precision=None):
    return lax.dot_general(a, b, (((1,), (1,)), ((), ())), preferred_element_type=F32, precision=precision)


def _split_bf16(a):
    hi = a.astype(BF16)
    return hi, (a - hi.astype(F32)).astype(BF16)


def _dot3(a_parts, b_parts):
    (ah, al), (bh, bl) = a_parts, b_parts
    return _dot(ah, bh) + (_dot(ah, bl) + _dot(al, bh))


def _inproj_kernel(x_ref, nw_ref, w_ref, wab_ref, o_ref, oab_ref, h_scr):
    @pl.when(pl.program_id(1) == 0)
    def _():
        hb = _rms_rows(x_ref[...], nw_ref[...]).astype(BF16)
        h_scr[...] = hb
        oab_ref[...] = _dot(hb, wab_ref[...])

    o_ref[...] = _dot(h_scr[...], w_ref[...])


def _inproj(x, nw, w_main, w_ab, tm):
    m, d = x.shape
    n = w_main.shape[1]
    tn = TN_INPROJ
    return pl.pallas_call(
        _inproj_kernel,
        grid=(m // tm, n // tn),
        in_specs=[pl.BlockSpec((tm, d), lambda i, j: (i, 0)),
                  pl.BlockSpec((1, d), lambda i, j: (0, 0)),
                  pl.BlockSpec((d, tn), lambda i, j: (0, j)),
                  pl.BlockSpec((d, HEAD_DIM), lambda i, j: (0, 0))],
        out_specs=[pl.BlockSpec((tm, tn), lambda i, j: (i, j)),
                   pl.BlockSpec((tm, HEAD_DIM), lambda i, j: (i, 0))],
        out_shape=[jax.ShapeDtypeStruct((m, n), F32), jax.ShapeDtypeStruct((m, HEAD_DIM), F32)],
        scratch_shapes=[pltpu.VMEM((tm, d), BF16)],
        compiler_params=_cparams("parallel", "arbitrary"),
        name="inproj",
    )(x, nw, w_main, w_ab)


def _outproj_kernel(x_ref, a_ref, b_ref, wa_ref, wb_ref, o_ref):
    o_ref[...] = x_ref[...] + (_dot(a_ref[...], wa_ref[...]) + _dot(b_ref[...], wb_ref[...]))


def _outproj(x, mix_dn, mix_mb, w_dn, w_mb, tm):
    m, d = x.shape
    k = mix_dn.shape[1]
    tn = TN_OUTPROJ
    return pl.pallas_call(
        _outproj_kernel,
        grid=(m // tm, d // tn),
        in_specs=[pl.BlockSpec((tm, tn), lambda i, j: (i, j)),
                  pl.BlockSpec((tm, k), lambda i, j: (i, 0)),
                  pl.BlockSpec((tm, k), lambda i, j: (i, 0)),
                  pl.BlockSpec((k, tn), lambda i, j: (0, j)),
                  pl.BlockSpec((k, tn), lambda i, j: (0, j))],
        out_specs=pl.BlockSpec((tm, tn), lambda i, j: (i, j)),
        out_shape=jax.ShapeDtypeStruct((m, d), F32),
        compiler_params=_cparams("parallel", "arbitrary"),
        name="outproj",
    )(x, mix_dn, mix_mb, w_dn, w_mb)


def _ffn_up_kernel(*refs, n_mean_steps, n_col_steps):
    if n_mean_steps:
        refs = refs[1:]
    x_ref, nw_ref, wg_ref, wu_ref = refs[:4]
    if n_mean_steps:
        pages = refs[4:4 + PAGES_PER_STEP]
        o_ref, km_ref, h_scr = refs[4 + PAGES_PER_STEP:]
    else:
        o_ref, h_scr = refs[4:]

    @pl.when(pl.program_id(1) == 0)
    def _():
        h_scr[...] = _rms_rows(x_ref[...], nw_ref[...]).astype(BF16)

    h = h_scr[...]
    g = _dot(h, wg_ref[...])
    u = _dot(h, wu_ref[...])
    o_ref[...] = (g * jax.nn.sigmoid(g) * u).astype(BF16)

    if n_mean_steps:
        @pl.when(pl.program_id(0) * n_col_steps + pl.program_id(1) < n_mean_steps)
        def _():
            pages_per_block = PAGES_PER_STEP // SUBLANES
            for n in range(SUBLANES):
                tot = jnp.sum(pages[n * pages_per_block][...], axis=0)
                for p in range(1, pages_per_block):
                    tot = tot + jnp.sum(pages[n * pages_per_block + p][...], axis=0)
                km_ref[n] = tot * (1.0 / MOBA_BLOCK)


def _ffn_up(x, nw, wg, wu, tm, cache_k=None, page_table=None, layer=None):
    m, d = x.shape
    n = wg.shape[1]
    tn = TN_FFN_UP
    grid = (m // tm, n // tn)
    in_specs = [pl.BlockSpec((tm, d), lambda i, j, *_: (i, 0)),
                pl.BlockSpec((1, d), lambda i, j, *_: (0, 0)),
                pl.BlockSpec((d, tn), lambda i, j, *_: (0, j)),
                pl.BlockSpec((d, tn), lambda i, j, *_: (0, j))]
    out_specs = [pl.BlockSpec((tm, tn), lambda i, j, *_: (i, j))]
    out_shape = [jax.ShapeDtypeStruct((m, n), BF16)]
    operands = [x, nw, wg, wu]
    n_mean_steps = 0
    if cache_k is not None:
        _, _, page, n_heads, hd = cache_k.shape
        n_seq, n_pages = page_table.shape
        assert MOBA_BLOCK % page == 0 and PAGES_PER_STEP * page == SUBLANES * MOBA_BLOCK
        steps_per_seq = n_pages // PAGES_PER_STEP
        n_mean_steps = n_seq * steps_per_seq
        assert n_pages % PAGES_PER_STEP == 0 and n_mean_steps <= grid[0] * grid[1]

        def mean_step(i, j):
            ks = jnp.minimum(i * grid[1] + j, n_mean_steps - 1)
            return ks // steps_per_seq, ks % steps_per_seq

        def page_spec(p):
            def index(i, j, pt):
                b, nb = mean_step(i, j)
                return (layer, pt[b, nb * PAGES_PER_STEP + p], 0, 0, 0)
            return pl.BlockSpec((None, None, page, n_heads, hd), index)

        in_specs += [page_spec(p) for p in range(PAGES_PER_STEP)]
        out_specs.append(pl.BlockSpec((None, SUBLANES, n_heads, hd), lambda i, j, pt: (*mean_step(i, j), 0, 0)))
        out_shape.append(jax.ShapeDtypeStruct((n_seq, n_pages * page // MOBA_BLOCK, n_heads, hd), F32))
        operands = [page_table] + operands + [cache_k] * PAGES_PER_STEP
    res = pl.pallas_call(
        functools.partial(_ffn_up_kernel, n_mean_steps=n_mean_steps, n_col_steps=grid[1]),
        grid_spec=pltpu.PrefetchScalarGridSpec(
            num_scalar_prefetch=1 if n_mean_steps else 0, grid=grid, in_specs=in_specs, out_specs=out_specs,
            scratch_shapes=[pltpu.VMEM((tm, d), BF16)]),
        out_shape=out_shape,
        compiler_params=_cparams("arbitrary", "arbitrary"),
        name="ffn_up",
    )(*operands)
    return res if n_mean_steps else res[0]


def _ffn_down_kernel(x_ref, a_ref, w_ref, o_ref):
    o_ref[...] = x_ref[...] + _dot(a_ref[...], w_ref[...])


def _ffn_down(x, act, wd, tm):
    m, d = x.shape
    k = act.shape[1]
    tn = TN_FFN_DOWN
    return pl.pallas_call(
        _ffn_down_kernel,
        grid=(m // tm, d // tn),
        in_specs=[pl.BlockSpec((tm, tn), lambda i, j: (i, j)),
                  pl.BlockSpec((tm, k), lambda i, j: (i, 0)),
                  pl.BlockSpec((k, tn), lambda i, j: (0, j))],
        out_specs=pl.BlockSpec((tm, tn), lambda i, j: (i, j)),
        out_shape=jax.ShapeDtypeStruct((m, d), F32),
        compiler_params=_cparams("parallel", "arbitrary"),
        name="ffn_down",
    )(x, act, wd)


def _final_norm_kernel(x_ref, nw_ref, o_ref):
    o_ref[...] = _rms_rows(x_ref[...], nw_ref[...])


def _final_norm(x, nw, tm):
    m, d = x.shape
    return pl.pallas_call(
        _final_norm_kernel,
        grid=(m // tm,),
        in_specs=[pl.BlockSpec((tm, d), lambda i: (i, 0)), pl.BlockSpec((1, d), lambda i: (0, 0))],
        out_specs=pl.BlockSpec((tm, d), lambda i: (i, 0)),
        out_shape=jax.ShapeDtypeStruct((m, d), F32),
        compiler_params=_cparams("parallel"),
        name="final_norm",
    )(x, nw)


def _deltanet_kernel(q_ref, k_ref, v_ref, z_ref, ab_ref, pq_ref, pk_ref, pv_ref, cq_ref, ck_ref, cv_ref,
                     alog_ref, dtb_ref, gn_ref, s0_ref, o_ref, s_ref,
                     u_scr, wq_scr, kdt_scr, attn_scr, egl_scr, *, t_rows, t_valid):
    c = DN_CHUNK
    hd = HEAD_DIM
    heads = range(DN_HEADS_PER_STEP)
    head0 = pl.program_id(1) * DN_HEADS_PER_STEP
    n_chunks = max(t_rows // c, 1)
    rows = min(t_rows, c)
    n_doublings = max((min(c, t_valid) - 1).bit_length() - 1, 0)
    group = math.gcd(n_chunks, max(DN_CHAINS // DN_HEADS_PER_STEP, 1))

    ii = lax.broadcasted_iota(jnp.int32, (c, c), 0)
    jj = lax.broadcasted_iota(jnp.int32, (c, c), 1)
    tril = ii >= jj
    strict = ii > jj
    tril_f = tril.astype(F32)
    eye_f = (ii == jj).astype(F32)
    ones_cc = jnp.ones((c, c), F32)
    lane = lax.broadcasted_iota(jnp.int32, (c, hd), 1)
    row = lax.broadcasted_iota(jnp.int32, (c, hd), 0)
    neg_decay_rate = -jnp.exp(alog_ref[...])
    dtb = dtb_ref[...]
    gn = gn_ref[...]

    def each(fn, *lists):
        return [fn(*args) for args in zip(*lists)]

    def cols(hh):
        return slice(hh * hd, (hh + 1) * hd)

    def load(ref, r0, hh):
        x = ref[pl.ds(r0, rows), cols(hh)]
        if rows < c:
            x = jnp.concatenate([x, jnp.zeros((c - rows, hd), F32)], axis=0)
        return x

    def conv_silu(ref, prev_ref, taps_ref, ci, hh):
        r0 = pl.multiple_of(ci * rows, rows)
        x = load(ref, r0, hh)
        tail = prev_ref[:, cols(hh)]
        if n_chunks > 1:
            before = ref[pl.ds(pl.multiple_of(jnp.maximum(r0 - SUBLANES, 0), SUBLANES), SUBLANES), cols(hh)]
            tail = jnp.where(ci == 0, tail, before)
        win = jnp.concatenate([tail, x], axis=0)
        taps = taps_ref[:, cols(hh)]
        y = win[5:5 + c] * taps[0:1]
        for i in range(1, CONV_W):
            y = y + win[5 + i:5 + i + c] * taps[i:i + 1]
        return y * jax.nn.sigmoid(y)

    def l2n(x):
        return x * lax.rsqrt(jnp.sum(x * x, axis=-1, keepdims=True) + L2_EPS)

    def gates(ci, hh):
        ab = ab_ref[pl.ds(pl.multiple_of(ci * rows, rows), rows), :]
        if rows < c:
            ab = jnp.concatenate([ab, jnp.zeros((c - rows, hd), F32)], axis=0)
        z_ab = ab + dtb
        softplus = jnp.maximum(z_ab, 0.0) + jnp.log1p(jnp.exp(-jnp.abs(z_ab)))
        g_sel, b_sel = lane == head0 + hh, lane == N_HEADS + head0 + hh
        if t_valid < n_chunks * c:
            valid = (row + ci * c) < t_valid
            g_sel, b_sel = g_sel & valid, b_sel & valid
        g_all = jnp.where(g_sel, neg_decay_rate * softplus, 0.0)
        beta = jnp.sum(jnp.where(b_sel, jax.nn.sigmoid(ab), 0.0), axis=1, keepdims=True)
        return g_all, beta

    def prepare(gi, carry):
        cis = [gi * group + t for t in range(group) for _ in heads]
        hhs = [hh for _ in range(group) for hh in heads]
        q = each(lambda ci, hh: l2n(conv_silu(q_ref, pq_ref, cq_ref, ci, hh)) * (hd ** -0.5), cis, hhs)
        k = each(lambda ci, hh: l2n(conv_silu(k_ref, pk_ref, ck_ref, ci, hh)), cis, hhs)
        v = each(lambda ci, hh: conv_silu(v_ref, pv_ref, cv_ref, ci, hh), cis, hhs)
        g_all, beta = zip(*each(gates, cis, hhs))
        gc = each(lambda g: jnp.sum(_dot(tril_f, g, HI), axis=1, keepdims=True), g_all)
        gc_row = each(lambda g: _dot(ones_cc, eye_f * g, HI), gc)
        decay = each(lambda g, gr: jnp.where(tril, jnp.exp(jnp.where(tril, g - gr, 0.0)), 0.0), gc, gc_row)
        e_gc = each(jnp.exp, gc)
        g_last = each(lambda g: g[c - 1:c, :], gc)
        kb = each(lambda a, bt: a * bt, k, beta)
        vb = each(lambda a, bt: a * bt, v, beta)
        k16 = each(lambda a: a.astype(BF16), k)
        neg_l = each(lambda a, b16, dc: jnp.where(strict, -(_dot_nt(a.astype(BF16), b16) * dc), 0.0), kb, k16, decay)
        tinv = each(lambda m: eye_f + m, neg_l)
        p = each(_split_bf16, neg_l)
        for _ in range(n_doublings):
            p = each(lambda pp: _split_bf16(_dot3(pp, pp)), p)
            tinv = each(lambda t, pp: t + _dot3(_split_bf16(t), pp), tinv, p)
        uw = each(lambda t, a, b, e: _dot3(_split_bf16(t), _split_bf16(jnp.concatenate([a, b * e], axis=1))),
                  tinv, vb, kb, e_gc)
        attn = each(lambda a, b16, dc: jnp.where(tril, _dot_nt(a.astype(BF16), b16) * dc, 0.0).astype(BF16),
                    q, k16, decay)
        kdt = each(lambda a, gl, g: (a * jnp.exp(gl - g)).T.astype(BF16), k, g_last, gc)
        for n, (ci, hh) in enumerate(zip(cis, hhs)):
            c0 = pl.multiple_of(ci * c, c)
            u_scr[hh, pl.ds(c0, c), :] = uw[n][:, :hd]
            wq_scr[hh, pl.ds(2 * c0, c), :] = uw[n][:, hd:].astype(BF16)
            wq_scr[hh, pl.ds(2 * c0 + c, c), :] = (q[n] * e_gc[n]).astype(BF16)
            attn_scr[hh, pl.ds(c0, c), :] = attn[n]
            kdt_scr[hh, pl.ds(c0, c), :] = kdt[n]
            egl_scr[hh, pl.ds(pl.multiple_of(ci * SUBLANES, SUBLANES), SUBLANES), :] = jnp.broadcast_to(
                jnp.exp(g_last[n]), (SUBLANES, hd))
        return carry

    lax.fori_loop(0, n_chunks // group, prepare, 0)

    def recur(ci, states):
        r0 = pl.multiple_of(ci * rows, rows)
        c0 = pl.multiple_of(ci * c, c)
        e0 = pl.multiple_of(ci * SUBLANES, SUBLANES)
        r = each(lambda hh, s: _dot(wq_scr[hh, pl.ds(2 * c0, 2 * c), :], s.astype(BF16)), heads, states)
        v_new = each(lambda hh, rr: (u_scr[hh, pl.ds(c0, c), :] - rr[:c]).astype(BF16), heads, r)
        o = each(lambda hh, rr, vn: rr[c:] + _dot(attn_scr[hh, pl.ds(c0, c), :], vn), heads, r, v_new)
        new_states = each(lambda hh, s, vn: s * egl_scr[hh, pl.ds(e0, SUBLANES), :][0:1]
                          + _dot(kdt_scr[hh, pl.ds(c0, c), :], vn), heads, states, v_new)
        for hh in heads:
            on = o[hh] * lax.rsqrt(jnp.mean(o[hh] * o[hh], axis=-1, keepdims=True) + RMS_EPS) * gn
            zc = load(z_ref, r0, hh)
            out = (on * (zc * jax.nn.sigmoid(zc))).astype(o_ref.dtype)
            o_ref[pl.ds(r0, rows), cols(hh)] = out[:rows]
        return tuple(new_states)

    final = lax.fori_loop(0, n_chunks, recur, tuple(s0_ref[hh] for hh in heads))
    for hh in heads:
        s_ref[hh] = final[hh]


def _deltanet(proj, proj_ab, conv_prev, conv_w, alog_row, dtb_row, gn_row, s0, layer, n_seq, t_rows, t_valid):
    lp = layer if conv_prev.shape[0] > 1 else 0
    t_pad = max(t_rows // DN_CHUNK, 1) * DN_CHUNK
    n_chunks = t_pad // DN_CHUNK
    hps = DN_HEADS_PER_STEP
    width = hps * HEAD_DIM
    assert N_HEADS % hps == 0

    def col(off):
        return pl.BlockSpec((t_rows, width), lambda b, h: (b, off // hps + h))

    def prev(off):
        return pl.BlockSpec((None, None, SUBLANES, width), lambda b, h: (lp, b, 0, off // hps + h))

    def taps(off):
        return pl.BlockSpec((CONV_W, width), lambda b, h: (0, off // hps + h))

    row_spec = pl.BlockSpec((1, HEAD_DIM), lambda b, h: (0, 0))
    return pl.pallas_call(
        functools.partial(_deltanet_kernel, t_rows=t_rows, t_valid=t_valid),
        grid=(n_seq, N_HEADS // hps),
        in_specs=[col(QDN_BLK), col(KDN_BLK), col(VDN_BLK), col(Z_BLK),
                  pl.BlockSpec((t_rows, HEAD_DIM), lambda b, h: (b, 0)),
                  prev(QDN_BLK), prev(KDN_BLK), prev(VDN_BLK),
                  taps(QDN_BLK), taps(KDN_BLK), taps(VDN_BLK),
                  row_spec, row_spec, row_spec,
                  pl.BlockSpec((None, None, hps, HEAD_DIM, HEAD_DIM), lambda b, h: (lp, b, h, 0, 0))],
        out_specs=[pl.BlockSpec((t_rows, width), lambda b, h: (b, h)),
                   pl.BlockSpec((None, hps, HEAD_DIM, HEAD_DIM), lambda b, h: (b, h, 0, 0))],
        out_shape=[jax.ShapeDtypeStruct((n_seq * t_rows, GROUP_W), BF16),
                   jax.ShapeDtypeStruct((n_seq, N_HEADS, HEAD_DIM, HEAD_DIM), F32)],
        scratch_shapes=[pltpu.VMEM((hps, t_pad, HEAD_DIM), F32),
                        pltpu.VMEM((hps, 2 * t_pad, HEAD_DIM), BF16),
                        pltpu.VMEM((hps, t_pad, DN_CHUNK), BF16),
                        pltpu.VMEM((hps, t_pad, DN_CHUNK), BF16),
                        pltpu.VMEM((hps, n_chunks * SUBLANES, HEAD_DIM), F32)],
        compiler_params=_cparams("parallel", "arbitrary"),
        name="deltanet",
    )(proj, proj, proj, proj, proj_ab, conv_prev, conv_prev, conv_prev, conv_w, conv_w, conv_w,
      alog_row, dtb_row, gn_row, s0)


def _rel_bias_tile(tbl_ref, h, dist):
    val = jnp.full(dist.shape, tbl_ref[h, 0], F32)
    for b in range(1, REL_BUCKETS):
        val = jnp.where(dist >= BUCKET_THR[b], tbl_ref[h, b], val)
    return val


def _gate_rank(gate, n_cand, idx, axis):
    rank = jnp.zeros(gate.shape, F32)
    for m in range(n_cand):
        gm = lax.slice_in_dim(gate, m, m + 1, axis=axis)
        ahead = (gm > gate) | ((gm == gate) & (m < idx))
        rank = rank + ahead.astype(F32)
    return rank


def _moba_prompt_kernel(tbl_ref, q_ref, k_ref, v_ref, o_ref, kmean_scr, bown_scr, bnear_scr, k16_scr, vt16_scr,
                        *, n_blocks):
    blk = MOBA_BLOCK
    hd = HEAD_DIM
    heads = range(MOBA_HEADS_PER_STEP)
    head0 = pl.program_id(0) * MOBA_HEADS_PER_STEP
    b = pl.program_id(1)
    j = pl.program_id(2)
    scale = hd ** -0.5

    def each(fn, *lists):
        return [fn(*args) for args in zip(*lists)]

    def cols(hh):
        return slice(hh * hd, (hh + 1) * hd)

    @pl.when((b == 0) & (j == 0))
    def _():
        ki = lax.broadcasted_iota(jnp.int32, (blk, blk), 0)
        qi = lax.broadcasted_iota(jnp.int32, (blk, blk), 1)
        d = qi - ki
        for hh in heads:
            bown_scr[hh] = jnp.where(d >= 0, _rel_bias_tile(tbl_ref, head0 + hh, d), -jnp.inf)
            bnear_scr[hh] = _rel_bias_tile(tbl_ref, head0 + hh, d + blk)

    @pl.when(j == 0)
    def _():
        for hh in heads:
            kmean_scr[hh] = jnp.zeros(kmean_scr.shape[1:], F32)
            for n in range(n_blocks):
                kmean_scr[hh, n:n + 1, :] = jnp.sum(k_ref[n * blk:(n + 1) * blk, cols(hh)], axis=0,
                                                    keepdims=True) * (1.0 / blk)
            k16_scr[hh] = k_ref[:, cols(hh)].astype(BF16)
            for n in range(n_blocks * blk // hd):
                rows = slice(n * hd, (n + 1) * hd)
                vt16_scr[hh, :, rows] = v_ref[rows, cols(hh)].T.astype(BF16)

    far_bias = [tbl_ref[head0 + hh, REL_BUCKETS - 1] for hh in heads]

    def attend(jb):
        q = [q_ref[:, cols(hh)] for hh in heads]
        q16 = each(lambda a: a.astype(BF16), q)
        n_keys = (jb + 1) * blk
        gated = jb > MOBA_TOPK
        if gated:
            cand = lax.broadcasted_iota(jnp.int32, (SUBLANES, blk), 0)
            gate_t = each(lambda hh, a: jnp.where(cand < jb, _dot_nt(kmean_scr[hh], a, HI)[:SUBLANES], -jnp.inf),
                          heads, q)
            keep_t = each(lambda g: (_gate_rank(g, jb, cand, 0) < MOBA_TOPK).astype(F32), gate_t)
        pieces = [[] for _ in heads]
        for n in range(jb + 1):
            for hh in heads:
                sn = _dot_nt(k16_scr[hh, n * blk:(n + 1) * blk, :], q16[hh]) * scale
                if n == jb:
                    sn = sn + bown_scr[hh]
                else:
                    sn = sn + (bnear_scr[hh] if n == jb - 1 else far_bias[hh])
                    if gated:
                        sn = jnp.where(keep_t[hh][n:n + 1, :] > 0.0, sn, -jnp.inf)
                pieces[hh].append(sn)

        def col_max(ps):
            m = jnp.max(ps[0], axis=0, keepdims=True)
            for sn in ps[1:]:
                m = jnp.maximum(m, jnp.max(sn, axis=0, keepdims=True))
            return m

        def col_sum(ps):
            tot = jnp.sum(ps[0], axis=0, keepdims=True)
            for pn in ps[1:]:
                tot = tot + jnp.sum(pn, axis=0, keepdims=True)
            return tot

        m = each(col_max, pieces)
        probs = each(lambda ps, mm: [jnp.exp(sn - mm) for sn in ps], pieces, m)
        denom = each(col_sum, probs)
        p = each(lambda ps: jnp.concatenate([pn.astype(BF16) for pn in ps], axis=0) if jb else ps[0].astype(BF16),
                 probs)
        o_t = each(lambda hh, pp, dn: _dot(vt16_scr[hh, :, 0:n_keys], pp) / dn, heads, p, denom)
        for hh in heads:
            o_ref[:, cols(hh)] = o_t[hh].T.astype(o_ref.dtype)

    for jb in range(n_blocks):
        pl.when(j == jb)(functools.partial(attend, jb))


def _moba_prompt(tbl, proj, n_seq, t_len):
    blk = MOBA_BLOCK
    n_blocks = t_len // blk
    hps = MOBA_HEADS_PER_STEP
    width = hps * HEAD_DIM
    assert MOBA_BLOCK >= BUCKET_THR[-1] and n_blocks <= SUBLANES and N_HEADS % hps == 0
    return pl.pallas_call(
        functools.partial(_moba_prompt_kernel, n_blocks=n_blocks),
        grid=(N_HEADS // hps, n_seq, n_blocks),
        in_specs=[pl.BlockSpec(memory_space=pltpu.SMEM),
                  pl.BlockSpec((blk, width), lambda h, b, j: (b * n_blocks + j, QMB_BLK // hps + h)),
                  pl.BlockSpec((t_len, width), lambda h, b, j: (b, KMB_BLK // hps + h)),
                  pl.BlockSpec((t_len, width), lambda h, b, j: (b, VMB_BLK // hps + h))],
        out_specs=pl.BlockSpec((blk, width), lambda h, b, j: (b * n_blocks + j, h)),
        out_shape=jax.ShapeDtypeStruct((n_seq * t_len, GROUP_W), BF16),
        scratch_shapes=[pltpu.VMEM((hps, HEAD_DIM, HEAD_DIM), F32),
                        pltpu.VMEM((hps, blk, blk), F32), pltpu.VMEM((hps, blk, blk), F32),
                        pltpu.VMEM((hps, t_len, HEAD_DIM), BF16), pltpu.VMEM((hps, HEAD_DIM, t_len), BF16)],
        compiler_params=_cparams("arbitrary", "arbitrary", "arbitrary"),
        name="moba_prompt",
    )(tbl, proj, proj, proj)


def _sample_select_kernel(q_ref, km_ref, o_ref, *, n_past_blocks):
    q = q_ref[...]
    km = km_ref[...]
    km = jnp.concatenate([km, jnp.zeros((HEAD_DIM - n_past_blocks, HEAD_DIM), F32)], axis=0)
    lane = lax.broadcasted_iota(jnp.int32, (SAMPLE_ROWS, HEAD_DIM), 1)
    gate = jnp.where(lane < n_past_blocks, _dot_nt(q, km, HI), -jnp.inf)
    rank = _gate_rank(gate, n_past_blocks, lane, 1)
    lane_f = lane.astype(F32)
    out = jnp.zeros((SAMPLE_ROWS, HEAD_DIM), F32)
    for r in range(MOBA_TOPK):
        idx = jnp.sum(jnp.where(rank == r, lane_f, 0.0), axis=1, keepdims=True)
        out = jnp.where(lane == r, idx, out)
    o_ref[...] = out.astype(jnp.int32)


def _sample_select(proj, kmean_t, n_seq):
    n_past_blocks = kmean_t.shape[2]
    assert MOBA_TOPK <= n_past_blocks <= HEAD_DIM
    return pl.pallas_call(
        functools.partial(_sample_select_kernel, n_past_blocks=n_past_blocks),
        grid=(n_seq, N_HEADS),
        in_specs=[pl.BlockSpec((SAMPLE_ROWS, HEAD_DIM), lambda b, h: (b, QMB_BLK + h)),
                  pl.BlockSpec((None, None, n_past_blocks, HEAD_DIM), lambda b, h: (b, h, 0, 0))],
        out_specs=pl.BlockSpec((None, None, SAMPLE_ROWS, HEAD_DIM), lambda b, h: (b, h, 0, 0)),
        out_shape=jax.ShapeDtypeStruct((n_seq, N_HEADS, SAMPLE_ROWS, HEAD_DIM), jnp.int32),
        compiler_params=_cparams("parallel", "arbitrary"),
        name="sample_select",
    )(proj, kmean_t)


def _moba_sample_kernel(pt_ref, sel_ref, tbl_ref, q_ref, kn_ref, vn_ref, ck_hbm, cv_hbm, o_ref, kbuf, vbuf, sem,
                        *, layer, n_valid, past_len, page):
    blk = MOBA_BLOCK
    pages_per_block = blk // page
    b, h = pl.program_id(0), pl.program_id(1)
    step = b * N_HEADS + h
    n_steps = pl.num_programs(0) * N_HEADS
    slot = step % 2
    scale = HEAD_DIM ** -0.5

    def page_copies(bb, hh, sl):
        copies = []
        for i in range(n_valid):
            for s in range(MOBA_TOPK):
                blk_idx = sel_ref[((bb * N_HEADS + hh) * n_valid + i) * MOBA_TOPK + s]
                for p in range(pages_per_block):
                    pg = pt_ref[bb, blk_idx * pages_per_block + p]
                    dst_rows = pl.ds(((i * MOBA_TOPK + s) * pages_per_block + p) * page, page)
                    copies.append(pltpu.make_async_copy(ck_hbm.at[layer, pg, :, hh, :],
                                                        kbuf.at[sl, dst_rows, :], sem.at[0, sl]))
                    copies.append(pltpu.make_async_copy(cv_hbm.at[layer, pg, :, hh, :],
                                                        vbuf.at[sl, dst_rows, :], sem.at[1, sl]))
        return copies

    @pl.when(step == 0)
    def _():
        for cp in page_copies(b, h, slot):
            cp.start()

    @pl.when(step + 1 < n_steps)
    def _():
        nxt = step + 1
        for cp in page_copies(nxt // N_HEADS, nxt % N_HEADS, 1 - slot):
            cp.start()

    for cp in page_copies(b, h, slot):
        cp.wait()

    qb = q_ref[...].astype(BF16)
    qrow = lax.broadcasted_iota(jnp.int32, (SAMPLE_ROWS, blk), 0)
    koff = lax.broadcasted_iota(jnp.int32, (SAMPLE_ROWS, blk), 1)
    orow = lax.broadcasted_iota(jnp.int32, (SAMPLE_ROWS, HEAD_DIM), 0)

    pad = jnp.zeros((HEAD_DIM - SAMPLE_ROWS, HEAD_DIM), F32)
    kn = jnp.concatenate([kn_ref[...], pad], axis=0).astype(BF16)
    vn = jnp.concatenate([vn_ref[...], pad], axis=0).astype(BF16)
    d_own = qrow[:, :HEAD_DIM] - koff[:, :HEAD_DIM]
    l_own = _dot_nt(qb, kn) * scale + _rel_bias_tile(tbl_ref, h, d_own)
    l_own = jnp.where(d_own >= 0, l_own, -jnp.inf)
    m_own = jnp.max(l_own, axis=1, keepdims=True)

    queries = range(n_valid)
    blocks = range(MOBA_TOPK)

    def score(i, s):
        sel = sel_ref[((b * N_HEADS + h) * n_valid + i) * MOBA_TOPK + s]
        dist = (past_len + qrow) - (sel * blk + koff)
        keys = kbuf[slot, pl.ds((i * MOBA_TOPK + s) * blk, blk), :].astype(BF16)
        return _dot_nt(qb, keys) * scale + _rel_bias_tile(tbl_ref, h, dist)

    logits = [[score(i, s) for s in blocks] for i in queries]
    m = []
    for i in queries:
        mi = m_own
        for l_s in logits[i]:
            mi = jnp.maximum(mi, jnp.max(l_s, axis=1, keepdims=True))
        m.append(mi)
    p_own = [jnp.exp(l_own - m[i]) for i in queries]
    p_sel = [[jnp.exp(logits[i][s] - m[i]) for s in blocks] for i in queries]
    result = jnp.zeros((SAMPLE_ROWS, HEAD_DIM), F32)
    for i in queries:
        denom = jnp.sum(p_own[i], axis=1, keepdims=True)
        acc = _dot(p_own[i].astype(BF16), vn)
        for s in blocks:
            denom = denom + jnp.sum(p_sel[i][s], axis=1, keepdims=True)
            values = vbuf[slot, pl.ds((i * MOBA_TOPK + s) * blk, blk), :].astype(BF16)
            acc = acc + _dot(p_sel[i][s].astype(BF16), values)
        result = jnp.where(orow == i, acc / denom, result)
    o_ref[...] = result.astype(o_ref.dtype)


def _moba_sample(page_table, sel_flat, tbl, proj, cache_k, cache_v, layer, n_seq, n_valid):
    page = cache_k.shape[2]
    past_len = page_table.shape[1] * page
    assert past_len % MOBA_BLOCK == 0 and MOBA_BLOCK % page == 0
    buf_rows = n_valid * MOBA_TOPK * MOBA_BLOCK

    def new_rows(off):
        return pl.BlockSpec((SAMPLE_ROWS, HEAD_DIM), lambda b, h, pt, sel: (b, off + h))

    return pl.pallas_call(
        functools.partial(_moba_sample_kernel, layer=layer, n_valid=n_valid, past_len=past_len, page=page),
        grid_spec=pltpu.PrefetchScalarGridSpec(
            num_scalar_prefetch=2,
            grid=(n_seq, N_HEADS),
            in_specs=[pl.BlockSpec(memory_space=pltpu.SMEM),
                      new_rows(QMB_BLK), new_rows(KMB_BLK), new_rows(VMB_BLK),
                      pl.BlockSpec(memory_space=pl.ANY), pl.BlockSpec(memory_space=pl.ANY)],
            out_specs=pl.BlockSpec((SAMPLE_ROWS, HEAD_DIM), lambda b, h, pt, sel: (b, h)),
            scratch_shapes=[pltpu.VMEM((2, buf_rows, HEAD_DIM), F32), pltpu.VMEM((2, buf_rows, HEAD_DIM), F32),
                            pltpu.SemaphoreType.DMA((2, 2))],
        ),
        out_shape=jax.ShapeDtypeStruct((n_seq * SAMPLE_ROWS, GROUP_W), BF16),
        compiler_params=_cparams("arbitrary", "arbitrary"),
        name="moba_sample",
    )(page_table, sel_flat, tbl, proj, proj, proj, cache_k, cache_v)


def _pad_row(v):
    return jnp.pad(v.astype(F32), (0, HEAD_DIM - v.shape[0])).reshape(1, HEAD_DIM)


def kernel(x_prompt, x_sample, cache_k, cache_v, page_table, state_conv, state_delta, w_in, conv_w, a_log, dt_bias,
           gn_w, w_out, norm_mix, norm_ffn, w_gate, w_up, w_down, rel_bias, final_norm):
    n_layers = w_in.shape[0]
    bp, t_len, d_model = x_prompt.shape
    bs, t_dec, _ = x_sample.shape
    conv_ch = 3 * GROUP_W
    a_off = conv_ch + GROUP_W
    mb_off = a_off + 2 * N_HEADS
    assert t_len % MOBA_BLOCK == 0 and CONV_W - 1 <= t_dec <= SAMPLE_ROWS
    assert cache_k.shape[3:] == (N_HEADS, HEAD_DIM)

    xp = x_prompt.reshape(bp * t_len, d_model)
    xs = jnp.pad(x_sample, ((0, 0), (0, SAMPLE_ROWS - t_dec), (0, 0))).reshape(bs * SAMPLE_ROWS, d_model)
    tm_p, tm_s = TM_PROMPT, bs * SAMPLE_ROWS

    tbl = rel_bias.astype(F32).T

    conv_prev_p = jnp.zeros((1, bp, SUBLANES, conv_ch), F32)
    conv_prev_s = jnp.pad(state_conv.astype(F32), ((0, 0), (0, 0), (SUBLANES - (CONV_W - 1), 0), (0, 0)))
    s0_p = jnp.zeros((1, bp, N_HEADS, HEAD_DIM, HEAD_DIM), F32)
    s0_s = state_delta.astype(F32)

    outs = {name: [] for name in ("kp", "vp", "cp", "sp", "kd", "vd", "cd", "sd")}
    for l in range(n_layers):
        w_main = jnp.concatenate([w_in[l][:, :a_off], w_in[l][:, mb_off:]], axis=1).astype(BF16)
        w_ab = jnp.pad(w_in[l][:, a_off:mb_off], ((0, 0), (0, HEAD_DIM - 2 * N_HEADS))).astype(BF16)
        w_out_dn = w_out[l][:GROUP_W].astype(BF16)
        w_out_mb = w_out[l][GROUP_W:].astype(BF16)
        wg, wu, wd = w_gate[l].astype(BF16), w_up[l].astype(BF16), w_down[l].astype(BF16)
        nm, nf = norm_mix[l].reshape(1, d_model), norm_ffn[l].reshape(1, d_model)
        alog_row, dtb_row, gn_row = _pad_row(a_log[l]), _pad_row(dt_bias[l]), gn_w[l].reshape(1, HEAD_DIM)

        proj, proj_ab = _inproj(xp, nm, w_main, w_ab, tm_p)
        mix_dn, s_new = _deltanet(proj, proj_ab, conv_prev_p, conv_w[l], alog_row, dtb_row, gn_row, s0_p, l,
                                  bp, t_len, t_len)
        mix_mb = _moba_prompt(tbl, proj, bp, t_len)
        xp = _outproj(xp, mix_dn, mix_mb, w_out_dn, w_out_mb, tm_p)
        act, kmean = _ffn_up(xp, nf, wg, wu, tm_p, cache_k, page_table, l)
        xp = _ffn_down(xp, act, wd, tm_p)
        kmean_t = kmean.transpose(0, 2, 1, 3)
        proj3 = proj.reshape(bp, t_len, MAIN_COLS)
        outs["kp"].append(proj3[:, :, KMB_BLK * HEAD_DIM:VMB_BLK * HEAD_DIM].reshape(bp, t_len, N_HEADS, HEAD_DIM))
        outs["vp"].append(proj3[:, :, VMB_BLK * HEAD_DIM:].reshape(bp, t_len, N_HEADS, HEAD_DIM))
        outs["cp"].append(proj3[:, t_len - (CONV_W - 1):, :conv_ch])
        outs["sp"].append(s_new)

        proj, proj_ab = _inproj(xs, nm, w_main, w_ab, tm_s)
        mix_dn, s_new = _deltanet(proj, proj_ab, conv_prev_s, conv_w[l], alog_row, dtb_row, gn_row, s0_s, l,
                                  bs, SAMPLE_ROWS, t_dec)
        sel = _sample_select(proj, kmean_t, bs)
        sel_flat = sel[:, :, :t_dec, :MOBA_TOPK].reshape(-1)
        mix_mb = _moba_sample(page_table, sel_flat, tbl, proj, cache_k, cache_v, l, bs, t_dec)
        xs = _outproj(xs, mix_dn, mix_mb, w_out_dn, w_out_mb, tm_s)
        xs = _ffn_down(xs, _ffn_up(xs, nf, wg, wu, tm_s), wd, tm_s)
        proj3 = proj.reshape(bs, SAMPLE_ROWS, MAIN_COLS)
        outs["kd"].append(proj3[:, :t_dec, KMB_BLK * HEAD_DIM:VMB_BLK * HEAD_DIM].reshape(bs, t_dec, N_HEADS, HEAD_DIM))
        outs["vd"].append(proj3[:, :t_dec, VMB_BLK * HEAD_DIM:].reshape(bs, t_dec, N_HEADS, HEAD_DIM))
        outs["cd"].append(proj3[:, t_dec - (CONV_W - 1):t_dec, :conv_ch])
        outs["sd"].append(s_new)

    fn = final_norm.reshape(1, d_model)
    y_prompt = _final_norm(xp, fn, tm_p).reshape(bp, t_len, d_model)
    y_sample = _final_norm(xs, fn, tm_s).reshape(bs, SAMPLE_ROWS, d_model)[:, :t_dec]
    return (y_prompt, y_sample) + tuple(jnp.stack(outs[n]) for n in ("kp", "vp", "cp", "sp", "kd", "vd", "cd", "sd"))
```

```python
import functools
import math

import jax
import jax.numpy as jnp
import numpy as np
from jax import lax
from jax.experimental import pallas as pl
from jax.experimental.pallas import tpu as pltpu

F32 = jnp.float32
BF16 = jnp.bfloat16
HI = lax.Precision.HIGHEST

HEAD_DIM = 128
N_HEADS = 8
GROUP_W = N_HEADS * HEAD_DIM
CONV_W = 4
DN_CHUNK = 128
DN_HEADS_PER_STEP = 2
DN_CHAINS = 4
MOBA_BLOCK = 256
MOBA_TOPK = 3
MOBA_HEADS_PER_STEP = 2
REL_BUCKETS = 32
REL_MAX_DIST = 128
RMS_EPS = 1e-6
L2_EPS = 1e-6
SUBLANES = 8
SAMPLE_ROWS = 8

QDN_BLK, KDN_BLK, VDN_BLK, Z_BLK, QMB_BLK, KMB_BLK, VMB_BLK = 0, 8, 16, 24, 32, 40, 48
MAIN_COLS = 56 * HEAD_DIM

TM_PROMPT = 1024
TN_INPROJ = 1024
TN_OUTPROJ = 1024
TN_FFN_UP = 512
TN_FFN_DOWN = 512
PAGES_PER_STEP = 16
TR_W_IN = 256
VMEM_LIMIT = 56 * 1024 * 1024


def _bucket_thresholds():
    d = np.arange(0, 4 * REL_MAX_DIST)
    max_exact = REL_BUCKETS // 2
    nf = np.maximum(d, 1).astype(np.float32)
    large = max_exact + (np.log(nf / np.float32(max_exact)) / np.float32(math.log(REL_MAX_DIST / max_exact))
                         * (REL_BUCKETS - max_exact)).astype(np.int32)
    bucket = np.where(d < max_exact, d, np.minimum(large, REL_BUCKETS - 1))
    return tuple(int(np.argmax(bucket >= b)) for b in range(REL_BUCKETS))


BUCKET_THR = _bucket_thresholds()


def _cparams(*sem):
    return pltpu.CompilerParams(dimension_semantics=sem, vmem_limit_bytes=VMEM_LIMIT)


def _rms_rows(x, w_row):
    return x * lax.rsqrt(jnp.mean(x * x, axis=-1, keepdims=True) + RMS_EPS) * w_row


def _dot(a, b, precision=None):
    return jnp.dot(a, b, preferred_element_type=F32, precision=precision)


def _dot_nt(a, b, precision=None):
    return lax.dot_general(a, b, (((1,), (1,)), ((), ())), preferred_element_type=F32, precision=precision)


def _split_bf16(a):
    hi = a.astype(BF16)
    return hi, (a - hi.astype(F32)).astype(BF16)


def _dot3(a_parts, b_parts):
    (ah, al), (bh, bl) = a_parts, b_parts
    return _dot(ah, bh) + (_dot(ah, bl) + _dot(al, bh))


def _split_w_in_kernel(w_ref, o_ref, oab_ref, *, a_off, mb_off):
    x = w_ref[...]
    o_ref[:, :a_off] = x[:, :a_off].astype(BF16)
    o_ref[:, a_off:] = x[:, mb_off:].astype(BF16)
    pad = jnp.zeros((x.shape[0], HEAD_DIM - (mb_off - a_off)), F32)
    oab_ref[...] = jnp.concatenate([x[:, a_off:mb_off], pad], axis=1).astype(BF16)


def _split_w_in(w_in, a_off, mb_off):
    n_layers, d, cols = w_in.shape
    assert a_off + (cols - mb_off) == MAIN_COLS and d % TR_W_IN == 0
    return pl.pallas_call(
        functools.partial(_split_w_in_kernel, a_off=a_off, mb_off=mb_off),
        grid=(n_layers, d // TR_W_IN),
        in_specs=[pl.BlockSpec((None, TR_W_IN, cols), lambda l, i: (l, i, 0))],
        out_specs=[pl.BlockSpec((None, TR_W_IN, MAIN_COLS), lambda l, i: (l, i, 0)),
                   pl.BlockSpec((None, TR_W_IN, HEAD_DIM), lambda l, i: (l, i, 0))],
        out_shape=[jax.ShapeDtypeStruct((n_layers, d, MAIN_COLS), BF16),
                   jax.ShapeDtypeStruct((n_layers, d, HEAD_DIM), BF16)],
        compiler_params=_cparams("parallel", "parallel"),
        name="split_w_in",
    )(w_in)


def _inproj_kernel(x_ref, nw_ref, w_ref, wab_ref, o_ref, oab_ref, h_scr):
    @pl.when(pl.program_id(1) == 0)
    def _():
        hb = _rms_rows(x_ref[...], nw_ref[...]).astype(BF16)
        h_scr[...] = hb
        oab_ref[...] = _dot(hb, wab_ref[...])

    o_ref[...] = _dot(h_scr[...], w_ref[...])


def _inproj(x, nw, w_main, w_ab, layer, tm):
    m, d = x.shape
    n = w_main.shape[2]
    tn = TN_INPROJ
    return pl.pallas_call(
        _inproj_kernel,
        grid=(m // tm, n // tn),
        in_specs=[pl.BlockSpec((tm, d), lambda i, j: (i, 0)),
                  pl.BlockSpec((1, d), lambda i, j: (0, 0)),
                  pl.BlockSpec((None, d, tn), lambda i, j: (layer, 0, j)),
                  pl.BlockSpec((None, d, HEAD_DIM), lambda i, j: (layer, 0, 0))],
        out_specs=[pl.BlockSpec((tm, tn), lambda i, j: (i, j)),
                   pl.BlockSpec((tm, HEAD_DIM), lambda i, j: (i, 0))],
        out_shape=[jax.ShapeDtypeStruct((m, n), F32), jax.ShapeDtypeStruct((m, HEAD_DIM), F32)],
        scratch_shapes=[pltpu.VMEM((tm, d), BF16)],
        compiler_params=_cparams("parallel", "arbitrary"),
        name="inproj",
    )(x, nw, w_main, w_ab)


def _outproj_kernel(x_ref, a_ref, b_ref, wa_ref, wb_ref, o_ref):
    o_ref[...] = x_ref[...] + (_dot(a_ref[...], wa_ref[...]) + _dot(b_ref[...], wb_ref[...]))


def _outproj(x, mix_dn, mix_mb, w_out, layer, tm):
    m, d = x.shape
    k = mix_dn.shape[1]
    tn = TN_OUTPROJ
    return pl.pallas_call(
        _outproj_kernel,
        grid=(m // tm, d // tn),
        in_specs=[pl.BlockSpec((tm, tn), lambda i, j: (i, j)),
                  pl.BlockSpec((tm, k), lambda i, j: (i, 0)),
                  pl.BlockSpec((tm, k), lambda i, j: (i, 0)),
                  pl.BlockSpec((None, k, tn), lambda i, j: (layer, 0, j)),
                  pl.BlockSpec((None, k, tn), lambda i, j: (layer, 1, j))],
        out_specs=pl.BlockSpec((tm, tn), lambda i, j: (i, j)),
        out_shape=jax.ShapeDtypeStruct((m, d), F32),
        compiler_params=_cparams("parallel", "arbitrary"),
        name="outproj",
    )(x, mix_dn, mix_mb, w_out, w_out)


def _ffn_up_kernel(*refs, n_mean_steps, n_col_steps):
    if n_mean_steps:
        refs = refs[1:]
    x_ref, nw_ref, wg_ref, wu_ref = refs[:4]
    if n_mean_steps:
        pages = refs[4:4 + PAGES_PER_STEP]
        o_ref, km_ref, h_scr = refs[4 + PAGES_PER_STEP:]
    else:
        o_ref, h_scr = refs[4:]

    @pl.when(pl.program_id(1) == 0)
    def _():
        h_scr[...] = _rms_rows(x_ref[...], nw_ref[...]).astype(BF16)

    h = h_scr[...]
    g = _dot(h, wg_ref[...])
    u = _dot(h, wu_ref[...])
    o_ref[...] = (g * jax.nn.sigmoid(g) * u).astype(BF16)

    if n_mean_steps:
        @pl.when(pl.program_id(0) * n_col_steps + pl.program_id(1) < n_mean_steps)
        def _():
            pages_per_block = PAGES_PER_STEP // SUBLANES
            for n in range(SUBLANES):
                tot = jnp.sum(pages[n * pages_per_block][...], axis=0)
                for p in range(1, pages_per_block):
                    tot = tot + jnp.sum(pages[n * pages_per_block + p][...], axis=0)
                km_ref[n] = tot * (1.0 / MOBA_BLOCK)


def _ffn_up(x, nw, wg, wu, layer, tm, cache_k=None, page_table=None):
    m, d = x.shape
    n = wg.shape[2]
    tn = TN_FFN_UP
    grid = (m // tm, n // tn)
    in_specs = [pl.BlockSpec((tm, d), lambda i, j, *_: (i, 0)),
                pl.BlockSpec((1, d), lambda i, j, *_: (0, 0)),
                pl.BlockSpec((None, d, tn), lambda i, j, *_: (layer, 0, j)),
                pl.BlockSpec((None, d, tn), lambda i, j, *_: (layer, 0, j))]
    out_specs = [pl.BlockSpec((tm, tn), lambda i, j, *_: (i, j))]
    out_shape = [jax.ShapeDtypeStruct((m, n), BF16)]
    operands = [x, nw, wg, wu]
    n_mean_steps = 0
    if cache_k is not None:
        _, _, page, n_heads, hd = cache_k.shape
        n_seq, n_pages = page_table.shape
        assert MOBA_BLOCK % page == 0 and PAGES_PER_STEP * page == SUBLANES * MOBA_BLOCK
        steps_per_seq = n_pages // PAGES_PER_STEP
        n_mean_steps = n_seq * steps_per_seq
        assert n_pages % PAGES_PER_STEP == 0 and n_mean_steps <= grid[0] * grid[1]

        def mean_step(i, j):
            ks = jnp.minimum(i * grid[1] + j, n_mean_steps - 1)
            return ks // steps_per_seq, ks % steps_per_seq

        def page_spec(p):
            def index(i, j, pt):
                b, nb = mean_step(i, j)
                return (layer, pt[b, nb * PAGES_PER_STEP + p], 0, 0, 0)
            return pl.BlockSpec((None, None, page, n_heads, hd), index)

        in_specs += [page_spec(p) for p in range(PAGES_PER_STEP)]
        out_specs.append(pl.BlockSpec((None, SUBLANES, n_heads, hd), lambda i, j, pt: (*mean_step(i, j), 0, 0)))
        out_shape.append(jax.ShapeDtypeStruct((n_seq, n_pages * page // MOBA_BLOCK, n_heads, hd), F32))
        operands = [page_table] + operands + [cache_k] * PAGES_PER_STEP
    res = pl.pallas_call(
        functools.partial(_ffn_up_kernel, n_mean_steps=n_mean_steps, n_col_steps=grid[1]),
        grid_spec=pltpu.PrefetchScalarGridSpec(
            num_scalar_prefetch=1 if n_mean_steps else 0, grid=grid, in_specs=in_specs, out_specs=out_specs,
            scratch_shapes=[pltpu.VMEM((tm, d), BF16)]),
        out_shape=out_shape,
        compiler_params=_cparams("arbitrary", "arbitrary"),
        name="ffn_up",
    )(*operands)
    return res if n_mean_steps else res[0]


def _ffn_down_kernel(x_ref, a_ref, w_ref, o_ref):
    o_ref[...] = x_ref[...] + _dot(a_ref[...], w_ref[...])


def _ffn_down(x, act, wd, layer, tm):
    m, d = x.shape
    k = act.shape[1]
    tn = TN_FFN_DOWN
    return pl.pallas_call(
        _ffn_down_kernel,
        grid=(m // tm, d // tn),
        in_specs=[pl.BlockSpec((tm, tn), lambda i, j: (i, j)),
                  pl.BlockSpec((tm, k), lambda i, j: (i, 0)),
                  pl.BlockSpec((None, k, tn), lambda i, j: (layer, 0, j))],
        out_specs=pl.BlockSpec((tm, tn), lambda i, j: (i, j)),
        out_shape=jax.ShapeDtypeStruct((m, d), F32),
        compiler_params=_cparams("parallel", "arbitrary"),
        name="ffn_down",
    )(x, act, wd)


def _final_norm_kernel(x_ref, nw_ref, o_ref):
    o_ref[...] = _rms_rows(x_ref[...], nw_ref[...])


def _final_norm(x, nw, tm):
    m, d = x.shape
    return pl.pallas_call(
        _final_norm_kernel,
        grid=(m // tm,),
        in_specs=[pl.BlockSpec((tm, d), lambda i: (i, 0)), pl.BlockSpec((1, d), lambda i: (0, 0))],
        out_specs=pl.BlockSpec((tm, d), lambda i: (i, 0)),
        out_shape=jax.ShapeDtypeStruct((m, d), F32),
        compiler_params=_cparams("parallel"),
        name="final_norm",
    )(x, nw)


def _deltanet_kernel(q_ref, k_ref, v_ref, z_ref, ab_ref, pq_ref, pk_ref, pv_ref, cq_ref, ck_ref, cv_ref,
                     alog_ref, dtb_ref, gn_ref, s0_ref, o_ref, s_ref,
                     u_scr, wq_scr, kdt_scr, attn_scr, egl_scr, *, t_rows, t_valid):
    c = DN_CHUNK
    hd = HEAD_DIM
    heads = range(DN_HEADS_PER_STEP)
    head0 = pl.program_id(1) * DN_HEADS_PER_STEP
    n_chunks = max(t_rows // c, 1)
    rows = min(t_rows, c)
    n_doublings = max((min(c, t_valid) - 1).bit_length() - 1, 0)
    group = math.gcd(n_chunks, max(DN_CHAINS // DN_HEADS_PER_STEP, 1))

    ii = lax.broadcasted_iota(jnp.int32, (c, c), 0)
    jj = lax.broadcasted_iota(jnp.int32, (c, c), 1)
    tril = ii >= jj
    strict = ii > jj
    tril_f = tril.astype(F32)
    eye_f = (ii == jj).astype(F32)
    ones_cc = jnp.ones((c, c), F32)
    lane = lax.broadcasted_iota(jnp.int32, (c, hd), 1)
    row = lax.broadcasted_iota(jnp.int32, (c, hd), 0)
    neg_decay_rate = -jnp.exp(alog_ref[...])
    dtb = dtb_ref[...]
    gn = gn_ref[...]

    def each(fn, *lists):
        return [fn(*args) for args in zip(*lists)]

    def cols(hh):
        return slice(hh * hd, (hh + 1) * hd)

    def load(ref, r0, hh):
        x = ref[pl.ds(r0, rows), cols(hh)]
        if rows < c:
            x = jnp.concatenate([x, jnp.zeros((c - rows, hd), F32)], axis=0)
        return x

    def conv_silu(ref, prev_ref, taps_ref, ci, hh):
        r0 = pl.multiple_of(ci * rows, rows)
        x = load(ref, r0, hh)
        tail = prev_ref[:, cols(hh)]
        if n_chunks > 1:
            before = ref[pl.ds(pl.multiple_of(jnp.maximum(r0 - SUBLANES, 0), SUBLANES), SUBLANES), cols(hh)]
            tail = jnp.where(ci == 0, tail, before)
        win = jnp.concatenate([tail, x], axis=0)
        taps = taps_ref[:, cols(hh)]
        y = win[5:5 + c] * taps[0:1]
        for i in range(1, CONV_W):
            y = y + win[5 + i:5 + i + c] * taps[i:i + 1]
        return y * jax.nn.sigmoid(y)

    def l2n(x):
        return x * lax.rsqrt(jnp.sum(x * x, axis=-1, keepdims=True) + L2_EPS)

    def gates(ci, hh):
        ab = ab_ref[pl.ds(pl.multiple_of(ci * rows, rows), rows), :]
        if rows < c:
            ab = jnp.concatenate([ab, jnp.zeros((c - rows, hd), F32)], axis=0)
        z_ab = ab + dtb
        softplus = jnp.maximum(z_ab, 0.0) + jnp.log1p(jnp.exp(-jnp.abs(z_ab)))
        g_sel, b_sel = lane == head0 + hh, lane == N_HEADS + head0 + hh
        if t_valid < n_chunks * c:
            valid = (row + ci * c) < t_valid
            g_sel, b_sel = g_sel & valid, b_sel & valid
        g_all = jnp.where(g_sel, neg_decay_rate * softplus, 0.0)
        beta = jnp.sum(jnp.where(b_sel, jax.nn.sigmoid(ab), 0.0), axis=1, keepdims=True)
        return g_all, beta

    def prepare(gi, carry):
        cis = [gi * group + t for t in range(group) for _ in heads]
        hhs = [hh for _ in range(group) for hh in heads]
        q = each(lambda ci, hh: l2n(conv_silu(q_ref, pq_ref, cq_ref, ci, hh)) * (hd ** -0.5), cis, hhs)
        k = each(lambda ci, hh: l2n(conv_silu(k_ref, pk_ref, ck_ref, ci, hh)), cis, hhs)
        v = each(lambda ci, hh: conv_silu(v_ref, pv_ref, cv_ref, ci, hh), cis, hhs)
        g_all, beta = zip(*each(gates, cis, hhs))
        gc = each(lambda g: jnp.sum(_dot(tril_f, g, HI), axis=1, keepdims=True), g_all)
        gc_row = each(lambda g: _dot(ones_cc, eye_f * g, HI), gc)
        decay = each(lambda g, gr: jnp.where(tril, jnp.exp(jnp.where(tril, g - gr, 0.0)), 0.0), gc, gc_row)
        e_gc = each(jnp.exp, gc)
        g_last = each(lambda g: g[c - 1:c, :], gc)
        kb = each(lambda a, bt: a * bt, k, beta)
        vb = each(lambda a, bt: a * bt, v, beta)
        k16 = each(lambda a: a.astype(BF16), k)
        neg_l = each(lambda a, b16, dc: jnp.where(strict, -(_dot_nt(a.astype(BF16), b16) * dc), 0.0), kb, k16, decay)
        tinv = each(lambda m: eye_f + m, neg_l)
        p = each(_split_bf16, neg_l)
        for _ in range(n_doublings):
            p = each(lambda pp: _split_bf16(_dot3(pp, pp)), p)
            tinv = each(lambda t, pp: t + _dot3(_split_bf16(t), pp), tinv, p)
        uw = each(lambda t, a, b, e: _dot3(_split_bf16(t), _split_bf16(jnp.concatenate([a, b * e], axis=1))),
                  tinv, vb, kb, e_gc)
        attn = each(lambda a, b16, dc: jnp.where(tril, _dot_nt(a.astype(BF16), b16) * dc, 0.0).astype(BF16),
                    q, k16, decay)
        kdt = each(lambda a, gl, g: (a * jnp.exp(gl - g)).T.astype(BF16), k, g_last, gc)
        for n, (ci, hh) in enumerate(zip(cis, hhs)):
            c0 = pl.multiple_of(ci * c, c)
            u_scr[hh, pl.ds(c0, c), :] = uw[n][:, :hd]
            wq_scr[hh, pl.ds(2 * c0, c), :] = uw[n][:, hd:].astype(BF16)
            wq_scr[hh, pl.ds(2 * c0 + c, c), :] = (q[n] * e_gc[n]).astype(BF16)
            attn_scr[hh, pl.ds(c0, c), :] = attn[n]
            kdt_scr[hh, pl.ds(c0, c), :] = kdt[n]
            egl_scr[hh, pl.ds(pl.multiple_of(ci * SUBLANES, SUBLANES), SUBLANES), :] = jnp.broadcast_to(
                jnp.exp(g_last[n]), (SUBLANES, hd))
        return carry

    lax.fori_loop(0, n_chunks // group, prepare, 0)

    def recur(ci, states):
        r0 = pl.multiple_of(ci * rows, rows)
        c0 = pl.multiple_of(ci * c, c)
        e0 = pl.multiple_of(ci * SUBLANES, SUBLANES)
        r = each(lambda hh, s: _dot(wq_scr[hh, pl.ds(2 * c0, 2 * c), :], s.astype(BF16)), heads, states)
        v_new = each(lambda hh, rr: (u_scr[hh, pl.ds(c0, c), :] - rr[:c]).astype(BF16), heads, r)
        o = each(lambda hh, rr, vn: rr[c:] + _dot(attn_scr[hh, pl.ds(c0, c), :], vn), heads, r, v_new)
        new_states = each(lambda hh, s, vn: s * egl_scr[hh, pl.ds(e0, SUBLANES), :][0:1]
                          + _dot(kdt_scr[hh, pl.ds(c0, c), :], vn), heads, states, v_new)
        for hh in heads:
            on = o[hh] * lax.rsqrt(jnp.mean(o[hh] * o[hh], axis=-1, keepdims=True) + RMS_EPS) * gn
            zc = load(z_ref, r0, hh)
            out = (on * (zc * jax.nn.sigmoid(zc))).astype(o_ref.dtype)
            o_ref[pl.ds(r0, rows), cols(hh)] = out[:rows]
        return tuple(new_states)

    final = lax.fori_loop(0, n_chunks, recur, tuple(s0_ref[hh] for hh in heads))
    for hh in heads:
        s_ref[hh] = final[hh]


def _deltanet(proj, proj_ab, conv_prev, conv_w, alog_row, dtb_row, gn_row, s0, layer, n_seq, t_rows, t_valid):
    lp = layer if conv_prev.shape[0] > 1 else 0
    t_pad = max(t_rows // DN_CHUNK, 1) * DN_CHUNK
    n_chunks = t_pad // DN_CHUNK
    hps = DN_HEADS_PER_STEP
    width = hps * HEAD_DIM
    assert N_HEADS % hps == 0

    def col(off):
        return pl.BlockSpec((t_rows, width), lambda b, h: (b, off // hps + h))

    def prev(off):
        return pl.BlockSpec((None, None, SUBLANES, width), lambda b, h: (lp, b, 0, off // hps + h))

    def taps(off):
        return pl.BlockSpec((CONV_W, width), lambda b, h: (0, off // hps + h))

    row_spec = pl.BlockSpec((1, HEAD_DIM), lambda b, h: (0, 0))
    return pl.pallas_call(
        functools.partial(_deltanet_kernel, t_rows=t_rows, t_valid=t_valid),
        grid=(n_seq, N_HEADS // hps),
        in_specs=[col(QDN_BLK), col(KDN_BLK), col(VDN_BLK), col(Z_BLK),
                  pl.BlockSpec((t_rows, HEAD_DIM), lambda b, h: (b, 0)),
                  prev(QDN_BLK), prev(KDN_BLK), prev(VDN_BLK),
                  taps(QDN_BLK), taps(KDN_BLK), taps(VDN_BLK),
                  row_spec, row_spec, row_spec,
                  pl.BlockSpec((None, None, hps, HEAD_DIM, HEAD_DIM), lambda b, h: (lp, b, h, 0, 0))],
        out_specs=[pl.BlockSpec((t_rows, width), lambda b, h: (b, h)),
                   pl.BlockSpec((None, hps, HEAD_DIM, HEAD_DIM), lambda b, h: (b, h, 0, 0))],
        out_shape=[jax.ShapeDtypeStruct((n_seq * t_rows, GROUP_W), BF16),
                   jax.ShapeDtypeStruct((n_seq, N_HEADS, HEAD_DIM, HEAD_DIM), F32)],
        scratch_shapes=[pltpu.VMEM((hps, t_pad, HEAD_DIM), F32),
                        pltpu.VMEM((hps, 2 * t_pad, HEAD_DIM), BF16),
                        pltpu.VMEM((hps, t_pad, DN_CHUNK), BF16),
                        pltpu.VMEM((hps, t_pad, DN_CHUNK), BF16),
                        pltpu.VMEM((hps, n_chunks * SUBLANES, HEAD_DIM), F32)],
        compiler_params=_cparams("parallel", "arbitrary"),
        name="deltanet",
    )(proj, proj, proj, proj, proj_ab, conv_prev, conv_prev, conv_prev, conv_w, conv_w, conv_w,
      alog_row, dtb_row, gn_row, s0)


def _rel_bias_tile(tbl_ref, h, dist):
    val = jnp.full(dist.shape, tbl_ref[h, 0], F32)
    for b in range(1, REL_BUCKETS):
        val = jnp.where(dist >= BUCKET_THR[b], tbl_ref[h, b], val)
    return val


def _gate_rank(gate, n_cand, idx, axis):
    rank = jnp.zeros(gate.shape, F32)
    for m in range(n_cand):
        gm = lax.slice_in_dim(gate, m, m + 1, axis=axis)
        ahead = (gm > gate) | ((gm == gate) & (m < idx))
        rank = rank + ahead.astype(F32)
    return rank


def _moba_prompt_kernel(tbl_ref, q_ref, k_ref, v_ref, o_ref, kmean_scr, bown_scr, bnear_scr, k16_scr, vt16_scr,
                        *, n_blocks):
    blk = MOBA_BLOCK
    hd = HEAD_DIM
    heads = range(MOBA_HEADS_PER_STEP)
    head0 = pl.program_id(0) * MOBA_HEADS_PER_STEP
    b = pl.program_id(1)
    j = pl.program_id(2)
    scale = hd ** -0.5

    def each(fn, *lists):
        return [fn(*args) for args in zip(*lists)]

    def cols(hh):
        return slice(hh * hd, (hh + 1) * hd)

    @pl.when((b == 0) & (j == 0))
    def _():
        ki = lax.broadcasted_iota(jnp.int32, (blk, blk), 0)
        qi = lax.broadcasted_iota(jnp.int32, (blk, blk), 1)
        d = qi - ki
        for hh in heads:
            bown_scr[hh] = jnp.where(d >= 0, _rel_bias_tile(tbl_ref, head0 + hh, d), -jnp.inf)
            bnear_scr[hh] = _rel_bias_tile(tbl_ref, head0 + hh, d + blk)

    @pl.when(j == 0)
    def _():
        for hh in heads:
            kmean_scr[hh] = jnp.zeros(kmean_scr.shape[1:], F32)
            for n in range(n_blocks):
                kmean_scr[hh, n:n + 1, :] = jnp.sum(k_ref[n * blk:(n + 1) * blk, cols(hh)], axis=0,
                                                    keepdims=True) * (1.0 / blk)
            k16_scr[hh] = k_ref[:, cols(hh)].astype(BF16)
            for n in range(n_blocks * blk // hd):
                rows = slice(n * hd, (n + 1) * hd)
                vt16_scr[hh, :, rows] = v_ref[rows, cols(hh)].T.astype(BF16)

    far_bias = [tbl_ref[head0 + hh, REL_BUCKETS - 1] for hh in heads]

    def attend(jb):
        q = [q_ref[:, cols(hh)] for hh in heads]
        q16 = each(lambda a: a.astype(BF16), q)
        n_keys = (jb + 1) * blk
        gated = jb > MOBA_TOPK
        if gated:
            cand = lax.broadcasted_iota(jnp.int32, (SUBLANES, blk), 0)
            gate_t = each(lambda hh, a: jnp.where(cand < jb, _dot_nt(kmean_scr[hh], a, HI)[:SUBLANES], -jnp.inf),
                          heads, q)
            keep_t = each(lambda g: (_gate_rank(g, jb, cand, 0) < MOBA_TOPK).astype(F32), gate_t)
        pieces = [[] for _ in heads]
        for n in range(jb + 1):
            for hh in heads:
                sn = _dot_nt(k16_scr[hh, n * blk:(n + 1) * blk, :], q16[hh]) * scale
                if n == jb:
                    sn = sn + bown_scr[hh]
                else:
                    sn = sn + (bnear_scr[hh] if n == jb - 1 else far_bias[hh])
                    if gated:
                        sn = jnp.where(keep_t[hh][n:n + 1, :] > 0.0, sn, -jnp.inf)
                pieces[hh].append(sn)

        def col_max(ps):
            m = jnp.max(ps[0], axis=0, keepdims=True)
            for sn in ps[1:]:
                m = jnp.maximum(m, jnp.max(sn, axis=0, keepdims=True))
            return m

        def col_sum(ps):
            tot = jnp.sum(ps[0], axis=0, keepdims=True)
            for pn in ps[1:]:
                tot = tot + jnp.sum(pn, axis=0, keepdims=True)
            return tot

        m = each(col_max, pieces)
        probs = each(lambda ps, mm: [jnp.exp(sn - mm) for sn in ps], pieces, m)
        denom = each(col_sum, probs)
        p = each(lambda ps: jnp.concatenate([pn.astype(BF16) for pn in ps], axis=0) if jb else ps[0].astype(BF16),
                 probs)
        o_t = each(lambda hh, pp, dn: _dot(vt16_scr[hh, :, 0:n_keys], pp) / dn, heads, p, denom)
        for hh in heads:
            o_ref[:, cols(hh)] = o_t[hh].T.astype(o_ref.dtype)

    for jb in range(n_blocks):
        pl.when(j == jb)(functools.partial(attend, jb))


def _moba_prompt(tbl, proj, n_seq, t_len):
    blk = MOBA_BLOCK
    n_blocks = t_len // blk
    hps = MOBA_HEADS_PER_STEP
    width = hps * HEAD_DIM
    assert MOBA_BLOCK >= BUCKET_THR[-1] and n_blocks <= SUBLANES and N_HEADS % hps == 0
    return pl.pallas_call(
        functools.partial(_moba_prompt_kernel, n_blocks=n_blocks),
        grid=(N_HEADS // hps, n_seq, n_blocks),
        in_specs=[pl.BlockSpec(memory_space=pltpu.SMEM),
                  pl.BlockSpec((blk, width), lambda h, b, j: (b * n_blocks + j, QMB_BLK // hps + h)),
                  pl.BlockSpec((t_len, width), lambda h, b, j: (b, KMB_BLK // hps + h)),
                  pl.BlockSpec((t_len, width), lambda h, b, j: (b, VMB_BLK // hps + h))],
        out_specs=pl.BlockSpec((blk, width), lambda h, b, j: (b * n_blocks + j, h)),
        out_shape=jax.ShapeDtypeStruct((n_seq * t_len, GROUP_W), BF16),
        scratch_shapes=[pltpu.VMEM((hps, HEAD_DIM, HEAD_DIM), F32),
                        pltpu.VMEM((hps, blk, blk), F32), pltpu.VMEM((hps, blk, blk), F32),
                        pltpu.VMEM((hps, t_len, HEAD_DIM), BF16), pltpu.VMEM((hps, HEAD_DIM, t_len), BF16)],
        compiler_params=_cparams("arbitrary", "arbitrary", "arbitrary"),
        name="moba_prompt",
    )(tbl, proj, proj, proj)


def _sample_select_kernel(q_ref, km_ref, o_ref, *, n_past_blocks):
    lane = lax.broadcasted_iota(jnp.int32, (SAMPLE_ROWS, HEAD_DIM), 1)
    lane_f = lane.astype(F32)
    pad = jnp.zeros((HEAD_DIM - n_past_blocks, HEAD_DIM), F32)
    gates = [jnp.where(lane < n_past_blocks,
                       _dot_nt(q_ref[:, hh * HEAD_DIM:(hh + 1) * HEAD_DIM],
                               jnp.concatenate([km_ref[hh], pad], axis=0), HI), -jnp.inf)
             for hh in range(N_HEADS)]
    for hh, gate in enumerate(gates):
        rank = _gate_rank(gate, n_past_blocks, lane, 1)
        out = jnp.zeros((SAMPLE_ROWS, HEAD_DIM), F32)
        for r in range(MOBA_TOPK):
            idx = jnp.sum(jnp.where(rank == r, lane_f, 0.0), axis=1, keepdims=True)
            out = jnp.where(lane == r, idx, out)
        o_ref[hh] = out.astype(jnp.int32)


def _sample_select(proj, kmean_t, n_seq):
    n_past_blocks = kmean_t.shape[2]
    assert MOBA_TOPK <= n_past_blocks <= HEAD_DIM
    return pl.pallas_call(
        functools.partial(_sample_select_kernel, n_past_blocks=n_past_blocks),
        grid=(n_seq,),
        in_specs=[pl.BlockSpec((SAMPLE_ROWS, GROUP_W), lambda b: (b, QMB_BLK // N_HEADS)),
                  pl.BlockSpec((None, N_HEADS, n_past_blocks, HEAD_DIM), lambda b: (b, 0, 0, 0))],
        out_specs=pl.BlockSpec((None, N_HEADS, SAMPLE_ROWS, HEAD_DIM), lambda b: (b, 0, 0, 0)),
        out_shape=jax.ShapeDtypeStruct((n_seq, N_HEADS, SAMPLE_ROWS, HEAD_DIM), jnp.int32),
        compiler_params=_cparams("arbitrary"),
        name="sample_select",
    )(proj, kmean_t)


def _moba_sample_kernel(pt_ref, sel_ref, tbl_ref, q_ref, kn_ref, vn_ref, ck_hbm, cv_hbm, o_ref, kbuf, vbuf, sem,
                        *, layer, n_valid, past_len, page):
    blk = MOBA_BLOCK
    pages_per_block = blk // page
    b, h = pl.program_id(0), pl.program_id(1)
    step = b * N_HEADS + h
    n_steps = pl.num_programs(0) * N_HEADS
    slot = step % 2
    scale = HEAD_DIM ** -0.5

    def page_copies(bb, hh, sl):
        copies = []
        for i in range(n_valid):
            for s in range(MOBA_TOPK):
                blk_idx = sel_ref[((bb * N_HEADS + hh) * n_valid + i) * MOBA_TOPK + s]
                for p in range(pages_per_block):
                    pg = pt_ref[bb, blk_idx * pages_per_block + p]
                    dst_rows = pl.ds(((i * MOBA_TOPK + s) * pages_per_block + p) * page, page)
                    copies.append(pltpu.make_async_copy(ck_hbm.at[layer, pg, :, hh, :],
                                                        kbuf.at[sl, dst_rows, :], sem.at[0, sl]))
                    copies.append(pltpu.make_async_copy(cv_hbm.at[layer, pg, :, hh, :],
                                                        vbuf.at[sl, dst_rows, :], sem.at[1, sl]))
        return copies

    @pl.when(step == 0)
    def _():
        for cp in page_copies(b, h, slot):
            cp.start()

    @pl.when(step + 1 < n_steps)
    def _():
        nxt = step + 1
        for cp in page_copies(nxt // N_HEADS, nxt % N_HEADS, 1 - slot):
            cp.start()

    for cp in page_copies(b, h, slot):
        cp.wait()

    qb = q_ref[...].astype(BF16)
    qrow = lax.broadcasted_iota(jnp.int32, (SAMPLE_ROWS, blk), 0)
    koff = lax.broadcasted_iota(jnp.int32, (SAMPLE_ROWS, blk), 1)
    orow = lax.broadcasted_iota(jnp.int32, (SAMPLE_ROWS, HEAD_DIM), 0)

    pad = jnp.zeros((HEAD_DIM - SAMPLE_ROWS, HEAD_DIM), F32)
    kn = jnp.concatenate([kn_ref[...], pad], axis=0).astype(BF16)
    vn = jnp.concatenate([vn_ref[...], pad], axis=0).astype(BF16)
    d_own = qrow[:, :HEAD_DIM] - koff[:, :HEAD_DIM]
    l_own = _dot_nt(qb, kn) * scale + _rel_bias_tile(tbl_ref, h, d_own)
    l_own = jnp.where(d_own >= 0, l_own, -jnp.inf)
    m_own = jnp.max(l_own, axis=1, keepdims=True)

    queries = range(n_valid)
    blocks = range(MOBA_TOPK)

    def score(i, s):
        sel = sel_ref[((b * N_HEADS + h) * n_valid + i) * MOBA_TOPK + s]
        dist = (past_len + qrow) - (sel * blk + koff)
        keys = kbuf[slot, pl.ds((i * MOBA_TOPK + s) * blk, blk), :].astype(BF16)
        return _dot_nt(qb, keys) * scale + _rel_bias_tile(tbl_ref, h, dist)

    logits = [[score(i, s) for s in blocks] for i in queries]
    m = []
    for i in queries:
        mi = m_own
        for l_s in logits[i]:
            mi = jnp.maximum(mi, jnp.max(l_s, axis=1, keepdims=True))
        m.append(mi)
    p_own = [jnp.exp(l_own - m[i]) for i in queries]
    p_sel = [[jnp.exp(logits[i][s] - m[i]) for s in blocks] for i in queries]
    result = jnp.zeros((SAMPLE_ROWS, HEAD_DIM), F32)
    for i in queries:
        denom = jnp.sum(p_own[i], axis=1, keepdims=True)
        acc = _dot(p_own[i].astype(BF16), vn)
        for s in blocks:
            denom = denom + jnp.sum(p_sel[i][s], axis=1, keepdims=True)
            values = vbuf[slot, pl.ds((i * MOBA_TOPK + s) * blk, blk), :].astype(BF16)
            acc = acc + _dot(p_sel[i][s].astype(BF16), values)
        result = jnp.where(orow == i, acc / denom, result)
    o_ref[...] = result.astype(o_ref.dtype)


def _moba_sample(page_table, sel_flat, tbl, proj, cache_k, cache_v, layer, n_seq, n_valid):
    page = cache_k.shape[2]
    past_len = page_table.shape[1] * page
    assert past_len % MOBA_BLOCK == 0 and MOBA_BLOCK % page == 0
    buf_rows = n_valid * MOBA_TOPK * MOBA_BLOCK

    def new_rows(off):
        return pl.BlockSpec((SAMPLE_ROWS, HEAD_DIM), lambda b, h, pt, sel: (b, off + h))

    return pl.pallas_call(
        functools.partial(_moba_sample_kernel, layer=layer, n_valid=n_valid, past_len=past_len, page=page),
        grid_spec=pltpu.PrefetchScalarGridSpec(
            num_scalar_prefetch=2,
            grid=(n_seq, N_HEADS),
            in_specs=[pl.BlockSpec(memory_space=pltpu.SMEM),
                      new_rows(QMB_BLK), new_rows(KMB_BLK), new_rows(VMB_BLK),
                      pl.BlockSpec(memory_space=pl.ANY), pl.BlockSpec(memory_space=pl.ANY)],
            out_specs=pl.BlockSpec((SAMPLE_ROWS, HEAD_DIM), lambda b, h, pt, sel: (b, h)),
            scratch_shapes=[pltpu.VMEM((2, buf_rows, HEAD_DIM), F32), pltpu.VMEM((2, buf_rows, HEAD_DIM), F32),
                            pltpu.SemaphoreType.DMA((2, 2))],
        ),
        out_shape=jax.ShapeDtypeStruct((n_seq * SAMPLE_ROWS, GROUP_W), BF16),
        compiler_params=_cparams("arbitrary", "arbitrary"),
        name="moba_sample",
    )(page_table, sel_flat, tbl, proj, proj, proj, cache_k, cache_v)


def _pad_row(v):
    return jnp.pad(v.astype(F32), (0, HEAD_DIM - v.shape[0])).reshape(1, HEAD_DIM)


def kernel(x_prompt, x_sample, cache_k, cache_v, page_table, state_conv, state_delta, w_in, conv_w, a_log, dt_bias,
           gn_w, w_out, norm_mix, norm_ffn, w_gate, w_up, w_down, rel_bias, final_norm):
    n_layers = w_in.shape[0]
    bp, t_len, d_model = x_prompt.shape
    bs, t_dec, _ = x_sample.shape
    conv_ch = 3 * GROUP_W
    a_off = conv_ch + GROUP_W
    mb_off = a_off + 2 * N_HEADS
    assert t_len % MOBA_BLOCK == 0 and CONV_W - 1 <= t_dec <= SAMPLE_ROWS
    assert cache_k.shape[3:] == (N_HEADS, HEAD_DIM)

    xp = x_prompt.reshape(bp * t_len, d_model)
    xs = jnp.pad(x_sample, ((0, 0), (0, SAMPLE_ROWS - t_dec), (0, 0))).reshape(bs * SAMPLE_ROWS, d_model)
    tm_p, tm_s = TM_PROMPT, bs * SAMPLE_ROWS

    tbl = rel_bias.astype(F32).T

    conv_prev_p = jnp.zeros((1, bp, SUBLANES, conv_ch), F32)
    conv_prev_s = jnp.pad(state_conv.astype(F32), ((0, 0), (0, 0), (SUBLANES - (CONV_W - 1), 0), (0, 0)))
    s0_p = jnp.zeros((1, bp, N_HEADS, HEAD_DIM, HEAD_DIM), F32)
    s0_s = state_delta.astype(F32)

    w_main, w_ab = _split_w_in(w_in, a_off, mb_off)
    w_out16, wg, wu, wd = w_out.astype(BF16), w_gate.astype(BF16), w_up.astype(BF16), w_down.astype(BF16)

    outs = {name: [] for name in ("kp", "vp", "cp", "sp", "kd", "vd", "cd", "sd")}
    for l in range(n_layers):
        nm, nf = norm_mix[l].reshape(1, d_model), norm_ffn[l].reshape(1, d_model)
        alog_row, dtb_row, gn_row = _pad_row(a_log[l]), _pad_row(dt_bias[l]), gn_w[l].reshape(1, HEAD_DIM)

        proj, proj_ab = _inproj(xp, nm, w_main, w_ab, l, tm_p)
        mix_dn, s_new = _deltanet(proj, proj_ab, conv_prev_p, conv_w[l], alog_row, dtb_row, gn_row, s0_p, l,
                                  bp, t_len, t_len)
        mix_mb = _moba_prompt(tbl, proj, bp, t_len)
        xp = _outproj(xp, mix_dn, mix_mb, w_out16, l, tm_p)
        act, kmean = _ffn_up(xp, nf, wg, wu, l, tm_p, cache_k, page_table)
        xp = _ffn_down(xp, act, wd, l, tm_p)
        kmean_t = kmean.transpose(0, 2, 1, 3)
        proj3 = proj.reshape(bp, t_len, MAIN_COLS)
        outs["kp"].append(proj3[:, :, KMB_BLK * HEAD_DIM:VMB_BLK * HEAD_DIM].reshape(bp, t_len, N_HEADS, HEAD_DIM))
        outs["vp"].append(proj3[:, :, VMB_BLK * HEAD_DIM:].reshape(bp, t_len, N_HEADS, HEAD_DIM))
        outs["cp"].append(proj3[:, t_len - (CONV_W - 1):, :conv_ch])
        outs["sp"].append(s_new)

        proj, proj_ab = _inproj(xs, nm, w_main, w_ab, l, tm_s)
        mix_dn, s_new = _deltanet(proj, proj_ab, conv_prev_s, conv_w[l], alog_row, dtb_row, gn_row, s0_s, l,
                                  bs, SAMPLE_ROWS, t_dec)
        sel = _sample_select(proj, kmean_t, bs)
        sel_flat = sel[:, :, :t_dec, :MOBA_TOPK].reshape(-1)
        mix_mb = _moba_sample(page_table, sel_flat, tbl, proj, cache_k, cache_v, l, bs, t_dec)
        xs = _outproj(xs, mix_dn, mix_mb, w_out16, l, tm_s)
        xs = _ffn_down(xs, _ffn_up(xs, nf, wg, wu, l, tm_s), wd, l, tm_s)
        proj3 = proj.reshape(bs, SAMPLE_ROWS, MAIN_COLS)
        outs["kd"].append(proj3[:, :t_dec, KMB_BLK * HEAD_DIM:VMB_BLK * HEAD_DIM].reshape(bs, t_dec, N_HEADS, HEAD_DIM))
        outs["vd"].append(proj3[:, :t_dec, VMB_BLK * HEAD_DIM:].reshape(bs, t_dec, N_HEADS, HEAD_DIM))
        outs["cd"].append(proj3[:, t_dec - (CONV_W - 1):t_dec, :conv_ch])
        outs["sd"].append(s_new)

    fn = final_norm.reshape(1, d_model)
    y_prompt = _final_norm(xp, fn, tm_p).reshape(bp, t_len, d_model)
    y_sample = _final_norm(xs, fn, tm_s).reshape(bs, SAMPLE_ROWS, d_model)[:, :t_dec]
    return (y_prompt, y_sample) + tuple(jnp.stack(outs[n]) for n in ("kp", "vp", "cp", "sp", "kd", "vd", "cd", "sd"))
```

```python
import functools
import math

import jax
import jax.numpy as jnp
import numpy as np
from jax import lax
from jax.experimental import pallas as pl
from jax.experimental.pallas import tpu as pltpu

F32 = jnp.float32
BF16 = jnp.bfloat16
HI = lax.Precision.HIGHEST

HEAD_DIM = 128
N_HEADS = 8
GROUP_W = N_HEADS * HEAD_DIM
CONV_W = 4
DN_CHUNK = 128
DN_HEADS_PER_STEP = 2
DN_CHAINS = 4
MOBA_BLOCK = 256
MOBA_TOPK = 3
MOBA_HEADS_PER_STEP = 2
REL_BUCKETS = 32
REL_MAX_DIST = 128
RMS_EPS = 1e-6
L2_EPS = 1e-6
SUBLANES = 8
SAMPLE_ROWS = 8

QDN_BLK, KDN_BLK, VDN_BLK, Z_BLK, QMB_BLK, KMB_BLK, VMB_BLK = 0, 8, 16, 24, 32, 40, 48
MAIN_COLS = 56 * HEAD_DIM

TM_PROMPT = 1024
TN_INPROJ = 1024
TN_OUTPROJ = 1024
TN_FFN_UP = 512
TN_FFN_DOWN = 512
PAGES_PER_STEP = 16
TC_W_IN = 512
VMEM_LIMIT = 56 * 1024 * 1024


def _bucket_thresholds():
    d = np.arange(0, 4 * REL_MAX_DIST)
    max_exact = REL_BUCKETS // 2
    nf = np.maximum(d, 1).astype(np.float32)
    large = max_exact + (np.log(nf / np.float32(max_exact)) / np.float32(math.log(REL_MAX_DIST / max_exact))
                         * (REL_BUCKETS - max_exact)).astype(np.int32)
    bucket = np.where(d < max_exact, d, np.minimum(large, REL_BUCKETS - 1))
    return tuple(int(np.argmax(bucket >= b)) for b in range(REL_BUCKETS))


BUCKET_THR = _bucket_thresholds()


def _cparams(*sem):
    return pltpu.CompilerParams(dimension_semantics=sem, vmem_limit_bytes=VMEM_LIMIT)


def _rms_rows(x, w_row):
    return x * lax.rsqrt(jnp.mean(x * x, axis=-1, keepdims=True) + RMS_EPS) * w_row


def _dot(a, b, precision=None):
    return jnp.dot(a, b, preferred_element_type=F32, precision=precision)


def _dot_nt(a, b, precision=None):
    return lax.dot_general(a, b, (((1,), (1,)), ((), ())), preferred_element_type=F32, precision=precision)


def _split_bf16(a):
    hi = a.astype(BF16)
    return hi, (a - hi.astype(F32)).astype(BF16)


def _dot3(a_parts, b_parts):
    (ah, al), (bh, bl) = a_parts, b_parts
    return _dot(ah, bh) + (_dot(ah, bl) + _dot(al, bh))


def _split_w_in_kernel(w_ref, ab_ref, o_ref, oab_ref):
    o_ref[...] = w_ref[0].T.astype(BF16)

    @pl.when(pl.program_id(1) == 0)
    def _():
        ab = ab_ref[0].T
        pad = jnp.zeros((ab.shape[0], HEAD_DIM - ab.shape[1]), F32)
        oab_ref[...] = jnp.concatenate([ab, pad], axis=1).astype(BF16)


def _split_w_in(w_in, a_off, mb_off):
    n_layers, d, cols = w_in.shape
    tc = TC_W_IN
    assert a_off + (cols - mb_off) == MAIN_COLS and a_off % tc == 0 and MAIN_COLS % tc == 0
    assert a_off % SUBLANES == 0 and mb_off % SUBLANES == 0
    w_t = jnp.swapaxes(w_in, 1, 2)

    def rows(l, c):
        return (l, pl.multiple_of(c * tc + jnp.where(c * tc >= a_off, mb_off - a_off, 0), SUBLANES), 0)

    return pl.pallas_call(
        _split_w_in_kernel,
        grid=(n_layers, MAIN_COLS // tc),
        in_specs=[pl.BlockSpec((pl.Element(1), pl.Element(tc), pl.Element(d)), rows),
                  pl.BlockSpec((pl.Element(1), pl.Element(mb_off - a_off), pl.Element(d)), lambda l, c: (l, a_off, 0))],
        out_specs=[pl.BlockSpec((None, d, tc), lambda l, c: (l, 0, c)),
                   pl.BlockSpec((None, d, HEAD_DIM), lambda l, c: (l, 0, 0))],
        out_shape=[jax.ShapeDtypeStruct((n_layers, d, MAIN_COLS), BF16),
                   jax.ShapeDtypeStruct((n_layers, d, HEAD_DIM), BF16)],
        compiler_params=_cparams("arbitrary", "arbitrary"),
        name="split_w_in",
    )(w_t, w_t)


def _inproj_kernel(x_ref, nw_ref, w_ref, wab_ref, alog_ref, dtb_ref, o_ref, ogb_ref, h_scr, *, seg, t_valid):
    @pl.when(pl.program_id(1) == 0)
    def _():
        hb = _rms_rows(x_ref[...], nw_ref[...]).astype(BF16)
        h_scr[...] = hb
        ab = _dot(hb, wab_ref[...])
        tm = ab.shape[0]
        ch = min(tm, DN_CHUNK)
        z_ab = ab + dtb_ref[...]
        softplus = jnp.maximum(z_ab, 0.0) + jnp.log1p(jnp.exp(-jnp.abs(z_ab)))
        g = -jnp.exp(alog_ref[...]) * softplus
        beta = jax.nn.sigmoid(ab)
        if t_valid < seg:
            valid = (lax.broadcasted_iota(jnp.int32, ab.shape, 0) & (seg - 1)) < t_valid
            g, beta = jnp.where(valid, g, 0.0), jnp.where(valid, beta, 0.0)
        ii = lax.broadcasted_iota(jnp.int32, (ch, ch), 0)
        jj = lax.broadcasted_iota(jnp.int32, (ch, ch), 1)
        same_run_before = ((ii >= jj) & ((ii - jj) <= (ii & (seg - 1)))).astype(F32)
        is_g = lax.broadcasted_iota(jnp.int32, (ch, HEAD_DIM), 1) < N_HEADS
        for c0 in range(0, tm, ch):
            gc = _dot(same_run_before, g[c0:c0 + ch], HI)
            ogb_ref[c0:c0 + ch, :] = jnp.where(is_g, gc, beta[c0:c0 + ch])

    o_ref[...] = _dot(h_scr[...], w_ref[...])


def _inproj(x, nw, w_main, w_ab, alog_row, dtb_row, layer, tm, seg, t_valid):
    m, d = x.shape
    n = w_main.shape[2]
    tn = TN_INPROJ
    assert seg & (seg - 1) == 0 and min(tm, DN_CHUNK) % seg == 0 and tm % min(tm, DN_CHUNK) == 0
    row_spec = pl.BlockSpec((1, HEAD_DIM), lambda i, j: (0, 0))
    return pl.pallas_call(
        functools.partial(_inproj_kernel, seg=seg, t_valid=t_valid),
        grid=(m // tm, n // tn),
        in_specs=[pl.BlockSpec((tm, d), lambda i, j: (i, 0)),
                  pl.BlockSpec((1, d), lambda i, j: (0, 0)),
                  pl.BlockSpec((None, d, tn), lambda i, j: (layer, 0, j)),
                  pl.BlockSpec((None, d, HEAD_DIM), lambda i, j: (layer, 0, 0)),
                  row_spec, row_spec],
        out_specs=[pl.BlockSpec((tm, tn), lambda i, j: (i, j)),
                   pl.BlockSpec((tm, HEAD_DIM), lambda i, j: (i, 0))],
        out_shape=[jax.ShapeDtypeStruct((m, n), F32), jax.ShapeDtypeStruct((m, HEAD_DIM), F32)],
        scratch_shapes=[pltpu.VMEM((tm, d), BF16)],
        compiler_params=_cparams("parallel", "arbitrary"),
        name="inproj",
    )(x, nw, w_main, w_ab, alog_row, dtb_row)


def _outproj_kernel(x_ref, a_ref, b_ref, wa_ref, wb_ref, o_ref):
    o_ref[...] = x_ref[...] + (_dot(a_ref[...], wa_ref[...]) + _dot(b_ref[...], wb_ref[...]))


def _outproj(x, mix_dn, mix_mb, w_out, layer, tm):
    m, d = x.shape
    k = mix_dn.shape[1]
    tn = TN_OUTPROJ
    return pl.pallas_call(
        _outproj_kernel,
        grid=(m // tm, d // tn),
        in_specs=[pl.BlockSpec((tm, tn), lambda i, j: (i, j)),
                  pl.BlockSpec((tm, k), lambda i, j: (i, 0)),
                  pl.BlockSpec((tm, k), lambda i, j: (i, 0)),
                  pl.BlockSpec((None, k, tn), lambda i, j: (layer, 0, j)),
                  pl.BlockSpec((None, k, tn), lambda i, j: (layer, 1, j))],
        out_specs=pl.BlockSpec((tm, tn), lambda i, j: (i, j)),
        out_shape=jax.ShapeDtypeStruct((m, d), F32),
        compiler_params=_cparams("parallel", "arbitrary"),
        name="outproj",
    )(x, mix_dn, mix_mb, w_out, w_out)


def _ffn_up_kernel(*refs, n_mean_steps, n_col_steps):
    if n_mean_steps:
        refs = refs[1:]
    x_ref, nw_ref, wg_ref, wu_ref = refs[:4]
    if n_mean_steps:
        pages = refs[4:4 + PAGES_PER_STEP]
        o_ref, km_ref, h_scr = refs[4 + PAGES_PER_STEP:]
    else:
        o_ref, h_scr = refs[4:]

    @pl.when(pl.program_id(1) == 0)
    def _():
        h_scr[...] = _rms_rows(x_ref[...], nw_ref[...]).astype(BF16)

    h = h_scr[...]
    g = _dot(h, wg_ref[...])
    u = _dot(h, wu_ref[...])
    o_ref[...] = (g * jax.nn.sigmoid(g) * u).astype(BF16)

    if n_mean_steps:
        @pl.when(pl.program_id(0) * n_col_steps + pl.program_id(1) < n_mean_steps)
        def _():
            pages_per_block = PAGES_PER_STEP // SUBLANES
            for n in range(SUBLANES):
                tot = jnp.sum(pages[n * pages_per_block][...], axis=0)
                for p in range(1, pages_per_block):
                    tot = tot + jnp.sum(pages[n * pages_per_block + p][...], axis=0)
                km_ref[n] = tot * (1.0 / MOBA_BLOCK)


def _ffn_up(x, nw, wg, wu, layer, tm, cache_k=None, page_table=None):
    m, d = x.shape
    n = wg.shape[2]
    tn = TN_FFN_UP
    grid = (m // tm, n // tn)
    in_specs = [pl.BlockSpec((tm, d), lambda i, j, *_: (i, 0)),
                pl.BlockSpec((1, d), lambda i, j, *_: (0, 0)),
                pl.BlockSpec((None, d, tn), lambda i, j, *_: (layer, 0, j)),
                pl.BlockSpec((None, d, tn), lambda i, j, *_: (layer, 0, j))]
    out_specs = [pl.BlockSpec((tm, tn), lambda i, j, *_: (i, j))]
    out_shape = [jax.ShapeDtypeStruct((m, n), BF16)]
    operands = [x, nw, wg, wu]
    n_mean_steps = 0
    if cache_k is not None:
        _, _, page, n_heads, hd = cache_k.shape
        n_seq, n_pages = page_table.shape
        assert MOBA_BLOCK % page == 0 and PAGES_PER_STEP * page == SUBLANES * MOBA_BLOCK
        steps_per_seq = n_pages // PAGES_PER_STEP
        n_mean_steps = n_seq * steps_per_seq
        assert n_pages % PAGES_PER_STEP == 0 and n_mean_steps <= grid[0] * grid[1]

        def mean_step(i, j):
            ks = jnp.minimum(i * grid[1] + j, n_mean_steps - 1)
            return ks // steps_per_seq, ks % steps_per_seq

        def page_spec(p):
            def index(i, j, pt):
                b, nb = mean_step(i, j)
                return (layer, pt[b, nb * PAGES_PER_STEP + p], 0, 0, 0)
            return pl.BlockSpec((None, None, page, n_heads, hd), index)

        in_specs += [page_spec(p) for p in range(PAGES_PER_STEP)]
        out_specs.append(pl.BlockSpec((None, SUBLANES, n_heads, hd), lambda i, j, pt: (*mean_step(i, j), 0, 0)))
        out_shape.append(jax.ShapeDtypeStruct((n_seq, n_pages * page // MOBA_BLOCK, n_heads, hd), F32))
        operands = [page_table] + operands + [cache_k] * PAGES_PER_STEP
    res = pl.pallas_call(
        functools.partial(_ffn_up_kernel, n_mean_steps=n_mean_steps, n_col_steps=grid[1]),
        grid_spec=pltpu.PrefetchScalarGridSpec(
            num_scalar_prefetch=1 if n_mean_steps else 0, grid=grid, in_specs=in_specs, out_specs=out_specs,
            scratch_shapes=[pltpu.VMEM((tm, d), BF16)]),
        out_shape=out_shape,
        compiler_params=_cparams("arbitrary", "arbitrary"),
        name="ffn_up",
    )(*operands)
    return res if n_mean_steps else res[0]


def _ffn_down_kernel(x_ref, a_ref, w_ref, o_ref):
    o_ref[...] = x_ref[...] + _dot(a_ref[...], w_ref[...])


def _ffn_down(x, act, wd, layer, tm):
    m, d = x.shape
    k = act.shape[1]
    tn = TN_FFN_DOWN
    return pl.pallas_call(
        _ffn_down_kernel,
        grid=(m // tm, d // tn),
        in_specs=[pl.BlockSpec((tm, tn), lambda i, j: (i, j)),
                  pl.BlockSpec((tm, k), lambda i, j: (i, 0)),
                  pl.BlockSpec((None, k, tn), lambda i, j: (layer, 0, j))],
        out_specs=pl.BlockSpec((tm, tn), lambda i, j: (i, j)),
        out_shape=jax.ShapeDtypeStruct((m, d), F32),
        compiler_params=_cparams("parallel", "arbitrary"),
        name="ffn_down",
    )(x, act, wd)


def _final_norm_kernel(x_ref, nw_ref, o_ref):
    o_ref[...] = _rms_rows(x_ref[...], nw_ref[...])


def _final_norm(x, nw, tm):
    m, d = x.shape
    return pl.pallas_call(
        _final_norm_kernel,
        grid=(m // tm,),
        in_specs=[pl.BlockSpec((tm, d), lambda i: (i, 0)), pl.BlockSpec((1, d), lambda i: (0, 0))],
        out_specs=pl.BlockSpec((tm, d), lambda i: (i, 0)),
        out_shape=jax.ShapeDtypeStruct((m, d), F32),
        compiler_params=_cparams("parallel"),
        name="final_norm",
    )(x, nw)


def _deltanet_kernel(q_ref, k_ref, v_ref, z_ref, gb_ref, pq_ref, pk_ref, pv_ref, cq_ref, ck_ref, cv_ref,
                     gn_ref, s0_ref, o_ref, s_ref,
                     u_scr, wq_scr, kdt_scr, attn_scr, egl_scr, *, t_rows, t_valid):
    c = DN_CHUNK
    hd = HEAD_DIM
    heads = range(DN_HEADS_PER_STEP)
    head0 = pl.program_id(1) * DN_HEADS_PER_STEP
    n_chunks = max(t_rows // c, 1)
    rows = min(t_rows, c)
    n_doublings = max((min(c, t_valid) - 1).bit_length() - 1, 0)
    group = math.gcd(n_chunks, max(DN_CHAINS // DN_HEADS_PER_STEP, 1))

    ii = lax.broadcasted_iota(jnp.int32, (c, c), 0)
    jj = lax.broadcasted_iota(jnp.int32, (c, c), 1)
    tril = ii >= jj
    strict = ii > jj
    eye_f = (ii == jj).astype(F32)
    ones_cc = jnp.ones((c, c), F32)
    lane = lax.broadcasted_iota(jnp.int32, (rows, hd), 1)
    gn = gn_ref[...]

    def each(fn, *lists):
        return [fn(*args) for args in zip(*lists)]

    def cols(hh):
        return slice(hh * hd, (hh + 1) * hd)

    def load(ref, r0, hh):
        x = ref[pl.ds(r0, rows), cols(hh)]
        if rows < c:
            x = jnp.concatenate([x, jnp.zeros((c - rows, hd), F32)], axis=0)
        return x

    def conv_silu(ref, prev_ref, taps_ref, ci, hh):
        r0 = pl.multiple_of(ci * rows, rows)
        x = load(ref, r0, hh)
        tail = prev_ref[:, cols(hh)]
        if n_chunks > 1:
            before = ref[pl.ds(pl.multiple_of(jnp.maximum(r0 - SUBLANES, 0), SUBLANES), SUBLANES), cols(hh)]
            tail = jnp.where(ci == 0, tail, before)
        win = jnp.concatenate([tail, x], axis=0)
        taps = taps_ref[:, cols(hh)]
        y = win[5:5 + c] * taps[0:1]
        for i in range(1, CONV_W):
            y = y + win[5 + i:5 + i + c] * taps[i:i + 1]
        return y * jax.nn.sigmoid(y)

    def l2n(x):
        return x * lax.rsqrt(jnp.sum(x * x, axis=-1, keepdims=True) + L2_EPS)

    def gates(ci, hh):
        gb = gb_ref[pl.ds(pl.multiple_of(ci * rows, rows), rows), :]
        gc = jnp.sum(jnp.where(lane == head0 + hh, gb, 0.0), axis=1, keepdims=True)
        beta = jnp.sum(jnp.where(lane == N_HEADS + head0 + hh, gb, 0.0), axis=1, keepdims=True)
        if rows < c:
            gc = jnp.concatenate([gc, jnp.broadcast_to(gc[rows - 1:rows], (c - rows, 1))], axis=0)
            beta = jnp.concatenate([beta, jnp.zeros((c - rows, 1), F32)], axis=0)
        return gc, beta

    def prepare(gi, carry):
        cis = [gi * group + t for t in range(group) for _ in heads]
        hhs = [hh for _ in range(group) for hh in heads]
        q = each(lambda ci, hh: l2n(conv_silu(q_ref, pq_ref, cq_ref, ci, hh)) * (hd ** -0.5), cis, hhs)
        k = each(lambda ci, hh: l2n(conv_silu(k_ref, pk_ref, ck_ref, ci, hh)), cis, hhs)
        v = each(lambda ci, hh: conv_silu(v_ref, pv_ref, cv_ref, ci, hh), cis, hhs)
        gc, beta = zip(*each(gates, cis, hhs))
        gc_row = each(lambda g: _dot(ones_cc, eye_f * g, HI), gc)
        decay = each(lambda g, gr: jnp.where(tril, jnp.exp(jnp.where(tril, g - gr, 0.0)), 0.0), gc, gc_row)
        e_gc = each(jnp.exp, gc)
        g_last = each(lambda g: g[c - 1:c, :], gc)
        kb = each(lambda a, bt: a * bt, k, beta)
        vb = each(lambda a, bt: a * bt, v, beta)
        k16 = each(lambda a: a.astype(BF16), k)
        neg_l = each(lambda a, b16, dc: jnp.where(strict, -(_dot_nt(a.astype(BF16), b16) * dc), 0.0), kb, k16, decay)
        tinv = each(lambda m: eye_f + m, neg_l)
        p = each(_split_bf16, neg_l)
        for _ in range(n_doublings):
            p = each(lambda pp: _split_bf16(_dot3(pp, pp)), p)
            tinv = each(lambda t, pp: t + _dot3(_split_bf16(t), pp), tinv, p)
        uw = each(lambda t, a, b, e: _dot3(_split_bf16(t), _split_bf16(jnp.concatenate([a, b * e], axis=1))),
                  tinv, vb, kb, e_gc)
        attn = each(lambda a, b16, dc: jnp.where(tril, _dot_nt(a.astype(BF16), b16) * dc, 0.0).astype(BF16),
                    q, k16, decay)
        kdt = each(lambda a, gl, g: (a * jnp.exp(gl - g)).T.astype(BF16), k, g_last, gc)
        for n, (ci, hh) in enumerate(zip(cis, hhs)):
            c0 = pl.multiple_of(ci * c, c)
            u_scr[hh, pl.ds(c0, c), :] = uw[n][:, :hd]
            wq_scr[hh, pl.ds(2 * c0, c), :] = uw[n][:, hd:].astype(BF16)
            wq_scr[hh, pl.ds(2 * c0 + c, c), :] = (q[n] * e_gc[n]).astype(BF16)
            attn_scr[hh, pl.ds(c0, c), :] = attn[n]
            kdt_scr[hh, pl.ds(c0, c), :] = kdt[n]
            egl_scr[hh, pl.ds(pl.multiple_of(ci * SUBLANES, SUBLANES), SUBLANES), :] = jnp.broadcast_to(
                jnp.exp(g_last[n]), (SUBLANES, hd))
        return carry

    lax.fori_loop(0, n_chunks // group, prepare, 0)

    def recur(ci, states):
        r0 = pl.multiple_of(ci * rows, rows)
        c0 = pl.multiple_of(ci * c, c)
        e0 = pl.multiple_of(ci * SUBLANES, SUBLANES)
        r = each(lambda hh, s: _dot(wq_scr[hh, pl.ds(2 * c0, 2 * c), :], s.astype(BF16)), heads, states)
        v_new = each(lambda hh, rr: (u_scr[hh, pl.ds(c0, c), :] - rr[:c]).astype(BF16), heads, r)
        o = each(lambda hh, rr, vn: rr[c:] + _dot(attn_scr[hh, pl.ds(c0, c), :], vn), heads, r, v_new)
        new_states = each(lambda hh, s, vn: s * egl_scr[hh, pl.ds(e0, SUBLANES), :][0:1]
                          + _dot(kdt_scr[hh, pl.ds(c0, c), :], vn), heads, states, v_new)
        for hh in heads:
            on = o[hh] * lax.rsqrt(jnp.mean(o[hh] * o[hh], axis=-1, keepdims=True) + RMS_EPS) * gn
            zc = load(z_ref, r0, hh)
            out = (on * (zc * jax.nn.sigmoid(zc))).astype(o_ref.dtype)
            o_ref[pl.ds(r0, rows), cols(hh)] = out[:rows]
        return tuple(new_states)

    final = lax.fori_loop(0, n_chunks, recur, tuple(s0_ref[hh] for hh in heads))
    for hh in heads:
        s_ref[hh] = final[hh]


def _deltanet(proj, gates, conv_prev, conv_w, gn_row, s0, layer, n_seq, t_rows, t_valid):
    lp = layer if conv_prev.shape[0] > 1 else 0
    t_pad = max(t_rows // DN_CHUNK, 1) * DN_CHUNK
    n_chunks = t_pad // DN_CHUNK
    hps = DN_HEADS_PER_STEP
    width = hps * HEAD_DIM
    assert N_HEADS % hps == 0

    def col(off):
        return pl.BlockSpec((t_rows, width), lambda b, h: (b, off // hps + h))

    def prev(off):
        return pl.BlockSpec((None, None, SUBLANES, width), lambda b, h: (lp, b, 0, off // hps + h))

    def taps(off):
        return pl.BlockSpec((CONV_W, width), lambda b, h: (0, off // hps + h))

    row_spec = pl.BlockSpec((1, HEAD_DIM), lambda b, h: (0, 0))
    return pl.pallas_call(
        functools.partial(_deltanet_kernel, t_rows=t_rows, t_valid=t_valid),
        grid=(n_seq, N_HEADS // hps),
        in_specs=[col(QDN_BLK), col(KDN_BLK), col(VDN_BLK), col(Z_BLK),
                  pl.BlockSpec((t_rows, HEAD_DIM), lambda b, h: (b, 0)),
                  prev(QDN_BLK), prev(KDN_BLK), prev(VDN_BLK),
                  taps(QDN_BLK), taps(KDN_BLK), taps(VDN_BLK),
                  row_spec,
                  pl.BlockSpec((None, None, hps, HEAD_DIM, HEAD_DIM), lambda b, h: (lp, b, h, 0, 0))],
        out_specs=[pl.BlockSpec((t_rows, width), lambda b, h: (b, h)),
                   pl.BlockSpec((None, hps, HEAD_DIM, HEAD_DIM), lambda b, h: (b, h, 0, 0))],
        out_shape=[jax.ShapeDtypeStruct((n_seq * t_rows, GROUP_W), BF16),
                   jax.ShapeDtypeStruct((n_seq, N_HEADS, HEAD_DIM, HEAD_DIM), F32)],
        scratch_shapes=[pltpu.VMEM((hps, t_pad, HEAD_DIM), F32),
                        pltpu.VMEM((hps, 2 * t_pad, HEAD_DIM), BF16),
                        pltpu.VMEM((hps, t_pad, DN_CHUNK), BF16),
                        pltpu.VMEM((hps, t_pad, DN_CHUNK), BF16),
                        pltpu.VMEM((hps, n_chunks * SUBLANES, HEAD_DIM), F32)],
        compiler_params=_cparams("parallel", "arbitrary"),
        name="deltanet",
    )(proj, proj, proj, proj, gates, conv_prev, conv_prev, conv_prev, conv_w, conv_w, conv_w, gn_row, s0)


def _rel_bias_tile(tbl_ref, h, dist):
    val = jnp.full(dist.shape, tbl_ref[h, 0], F32)
    for b in range(1, REL_BUCKETS):
        val = jnp.where(dist >= BUCKET_THR[b], tbl_ref[h, b], val)
    return val


def _gate_rank(gate, n_cand, idx, axis):
    rank = jnp.zeros(gate.shape, F32)
    for m in range(n_cand):
        gm = lax.slice_in_dim(gate, m, m + 1, axis=axis)
        ahead = (gm > gate) | ((gm == gate) & (m < idx))
        rank = rank + ahead.astype(F32)
    return rank


def _moba_prompt_kernel(tbl_ref, q_ref, k_ref, v_ref, o_ref, kmean_scr, bown_scr, bnear_scr, k16_scr, vt16_scr,
                        *, n_blocks):
    blk = MOBA_BLOCK
    hd = HEAD_DIM
    heads = range(MOBA_HEADS_PER_STEP)
    head0 = pl.program_id(0) * MOBA_HEADS_PER_STEP
    b = pl.program_id(1)
    j = pl.program_id(2)
    scale = hd ** -0.5

    def each(fn, *lists):
        return [fn(*args) for args in zip(*lists)]

    def cols(hh):
        return slice(hh * hd, (hh + 1) * hd)

    @pl.when((b == 0) & (j == 0))
    def _():
        ki = lax.broadcasted_iota(jnp.int32, (blk, blk), 0)
        qi = lax.broadcasted_iota(jnp.int32, (blk, blk), 1)
        d = qi - ki
        for hh in heads:
            bown_scr[hh] = jnp.where(d >= 0, _rel_bias_tile(tbl_ref, head0 + hh, d), -jnp.inf)
            bnear_scr[hh] = _rel_bias_tile(tbl_ref, head0 + hh, d + blk)

    @pl.when(j == 0)
    def _():
        for hh in heads:
            kmean_scr[hh] = jnp.zeros(kmean_scr.shape[1:], F32)
            for n in range(n_blocks):
                kmean_scr[hh, n:n + 1, :] = jnp.sum(k_ref[n * blk:(n + 1) * blk, cols(hh)], axis=0,
                                                    keepdims=True) * (1.0 / blk)
            k16_scr[hh] = k_ref[:, cols(hh)].astype(BF16)
            for n in range(n_blocks * blk // hd):
                rows = slice(n * hd, (n + 1) * hd)
                vt16_scr[hh, :, rows] = v_ref[rows, cols(hh)].T.astype(BF16)

    far_bias = [tbl_ref[head0 + hh, REL_BUCKETS - 1] for hh in heads]

    def attend(jb):
        q = [q_ref[:, cols(hh)] for hh in heads]
        q16 = each(lambda a: a.astype(BF16), q)
        n_keys = (jb + 1) * blk
        gated = jb > MOBA_TOPK
        if gated:
            cand = lax.broadcasted_iota(jnp.int32, (SUBLANES, blk), 0)
            gate_t = each(lambda hh, a: jnp.where(cand < jb, _dot_nt(kmean_scr[hh], a, HI)[:SUBLANES], -jnp.inf),
                          heads, q)
            keep_t = each(lambda g: (_gate_rank(g, jb, cand, 0) < MOBA_TOPK).astype(F32), gate_t)
        pieces = [[] for _ in heads]
        for n in range(jb + 1):
            for hh in heads:
                sn = _dot_nt(k16_scr[hh, n * blk:(n + 1) * blk, :], q16[hh]) * scale
                if n == jb:
                    sn = sn + bown_scr[hh]
                else:
                    sn = sn + (bnear_scr[hh] if n == jb - 1 else far_bias[hh])
                    if gated:
                        sn = jnp.where(keep_t[hh][n:n + 1, :] > 0.0, sn, -jnp.inf)
                pieces[hh].append(sn)

        def col_max(ps):
            m = jnp.max(ps[0], axis=0, keepdims=True)
            for sn in ps[1:]:
                m = jnp.maximum(m, jnp.max(sn, axis=0, keepdims=True))
            return m

        def col_sum(ps):
            tot = jnp.sum(ps[0], axis=0, keepdims=True)
            for pn in ps[1:]:
                tot = tot + jnp.sum(pn, axis=0, keepdims=True)
            return tot

        m = each(col_max, pieces)
        probs = each(lambda ps, mm: [jnp.exp(sn - mm) for sn in ps], pieces, m)
        denom = each(col_sum, probs)
        p = each(lambda ps: jnp.concatenate([pn.astype(BF16) for pn in ps], axis=0) if jb else ps[0].astype(BF16),
                 probs)
        o_t = each(lambda hh, pp, dn: _dot(vt16_scr[hh, :, 0:n_keys], pp) / dn, heads, p, denom)
        for hh in heads:
            o_ref[:, cols(hh)] = o_t[hh].T.astype(o_ref.dtype)

    for jb in range(n_blocks):
        pl.when(j == jb)(functools.partial(attend, jb))


def _moba_prompt(tbl, proj, n_seq, t_len):
    blk = MOBA_BLOCK
    n_blocks = t_len // blk
    hps = MOBA_HEADS_PER_STEP
    width = hps * HEAD_DIM
    assert MOBA_BLOCK >= BUCKET_THR[-1] and n_blocks <= SUBLANES and N_HEADS % hps == 0
    return pl.pallas_call(
        functools.partial(_moba_prompt_kernel, n_blocks=n_blocks),
        grid=(N_HEADS // hps, n_seq, n_blocks),
        in_specs=[pl.BlockSpec(memory_space=pltpu.SMEM),
                  pl.BlockSpec((blk, width), lambda h, b, j: (b * n_blocks + j, QMB_BLK // hps + h)),
                  pl.BlockSpec((t_len, width), lambda h, b, j: (b, KMB_BLK // hps + h)),
                  pl.BlockSpec((t_len, width), lambda h, b, j: (b, VMB_BLK // hps + h))],
        out_specs=pl.BlockSpec((blk, width), lambda h, b, j: (b * n_blocks + j, h)),
        out_shape=jax.ShapeDtypeStruct((n_seq * t_len, GROUP_W), BF16),
        scratch_shapes=[pltpu.VMEM((hps, HEAD_DIM, HEAD_DIM), F32),
                        pltpu.VMEM((hps, blk, blk), F32), pltpu.VMEM((hps, blk, blk), F32),
                        pltpu.VMEM((hps, t_len, HEAD_DIM), BF16), pltpu.VMEM((hps, HEAD_DIM, t_len), BF16)],
        compiler_params=_cparams("arbitrary", "arbitrary", "arbitrary"),
        name="moba_prompt",
    )(tbl, proj, proj, proj)


def _sample_select_kernel(q_ref, km_ref, o_ref, *, n_past_blocks):
    lane = lax.broadcasted_iota(jnp.int32, (SAMPLE_ROWS, HEAD_DIM), 1)
    lane_f = lane.astype(F32)
    pad = jnp.zeros((HEAD_DIM - n_past_blocks, HEAD_DIM), F32)
    gates = [jnp.where(lane < n_past_blocks,
                       _dot_nt(q_ref[:, hh * HEAD_DIM:(hh + 1) * HEAD_DIM],
                               jnp.concatenate([km_ref[hh], pad], axis=0), HI), -jnp.inf)
             for hh in range(N_HEADS)]
    for hh, gate in enumerate(gates):
        rank = _gate_rank(gate, n_past_blocks, lane, 1)
        out = jnp.zeros((SAMPLE_ROWS, HEAD_DIM), F32)
        for r in range(MOBA_TOPK):
            idx = jnp.sum(jnp.where(rank == r, lane_f, 0.0), axis=1, keepdims=True)
            out = jnp.where(lane == r, idx, out)
        o_ref[hh] = out.astype(jnp.int32)


def _sample_select(proj, kmean_t, n_seq):
    n_past_blocks = kmean_t.shape[2]
    assert MOBA_TOPK <= n_past_blocks <= HEAD_DIM
    return pl.pallas_call(
        functools.partial(_sample_select_kernel, n_past_blocks=n_past_blocks),
        grid=(n_seq,),
        in_specs=[pl.BlockSpec((SAMPLE_ROWS, GROUP_W), lambda b: (b, QMB_BLK // N_HEADS)),
                  pl.BlockSpec((None, N_HEADS, n_past_blocks, HEAD_DIM), lambda b: (b, 0, 0, 0))],
        out_specs=pl.BlockSpec((None, N_HEADS, SAMPLE_ROWS, HEAD_DIM), lambda b: (b, 0, 0, 0)),
        out_shape=jax.ShapeDtypeStruct((n_seq, N_HEADS, SAMPLE_ROWS, HEAD_DIM), jnp.int32),
        compiler_params=_cparams("arbitrary"),
        name="sample_select",
    )(proj, kmean_t)


def _moba_sample_kernel(pt_ref, sel_ref, tbl_ref, q_ref, kn_ref, vn_ref, ck_hbm, cv_hbm, o_ref, kbuf, vbuf, sem,
                        *, layer, n_valid, past_len, page):
    blk = MOBA_BLOCK
    pages_per_block = blk // page
    b, h = pl.program_id(0), pl.program_id(1)
    step = b * N_HEADS + h
    n_steps = pl.num_programs(0) * N_HEADS
    slot = step % 2
    scale = HEAD_DIM ** -0.5

    def page_copies(bb, hh, sl):
        copies = []
        for i in range(n_valid):
            for s in range(MOBA_TOPK):
                blk_idx = sel_ref[((bb * N_HEADS + hh) * n_valid + i) * MOBA_TOPK + s]
                for p in range(pages_per_block):
                    pg = pt_ref[bb, blk_idx * pages_per_block + p]
                    dst_rows = pl.ds(((i * MOBA_TOPK + s) * pages_per_block + p) * page, page)
                    copies.append(pltpu.make_async_copy(ck_hbm.at[layer, pg, :, hh, :],
                                                        kbuf.at[sl, dst_rows, :], sem.at[0, sl]))
                    copies.append(pltpu.make_async_copy(cv_hbm.at[layer, pg, :, hh, :],
                                                        vbuf.at[sl, dst_rows, :], sem.at[1, sl]))
        return copies

    @pl.when(step == 0)
    def _():
        for cp in page_copies(b, h, slot):
            cp.start()

    @pl.when(step + 1 < n_steps)
    def _():
        nxt = step + 1
        for cp in page_copies(nxt // N_HEADS, nxt % N_HEADS, 1 - slot):
            cp.start()

    for cp in page_copies(b, h, slot):
        cp.wait()

    qb = q_ref[...].astype(BF16)
    qrow = lax.broadcasted_iota(jnp.int32, (SAMPLE_ROWS, blk), 0)
    koff = lax.broadcasted_iota(jnp.int32, (SAMPLE_ROWS, blk), 1)
    orow = lax.broadcasted_iota(jnp.int32, (SAMPLE_ROWS, HEAD_DIM), 0)

    pad = jnp.zeros((HEAD_DIM - SAMPLE_ROWS, HEAD_DIM), F32)
    kn = jnp.concatenate([kn_ref[...], pad], axis=0).astype(BF16)
    vn = jnp.concatenate([vn_ref[...], pad], axis=0).astype(BF16)
    d_own = qrow[:, :HEAD_DIM] - koff[:, :HEAD_DIM]
    l_own = _dot_nt(qb, kn) * scale + _rel_bias_tile(tbl_ref, h, d_own)
    l_own = jnp.where(d_own >= 0, l_own, -jnp.inf)
    m_own = jnp.max(l_own, axis=1, keepdims=True)

    queries = range(n_valid)
    blocks = range(MOBA_TOPK)

    def score(i, s):
        sel = sel_ref[((b * N_HEADS + h) * n_valid + i) * MOBA_TOPK + s]
        dist = (past_len + qrow) - (sel * blk + koff)
        keys = kbuf[slot, pl.ds((i * MOBA_TOPK + s) * blk, blk), :].astype(BF16)
        return _dot_nt(qb, keys) * scale + _rel_bias_tile(tbl_ref, h, dist)

    logits = [[score(i, s) for s in blocks] for i in queries]
    m = []
    for i in queries:
        mi = m_own
        for l_s in logits[i]:
            mi = jnp.maximum(mi, jnp.max(l_s, axis=1, keepdims=True))
        m.append(mi)
    p_own = [jnp.exp(l_own - m[i]) for i in queries]
    p_sel = [[jnp.exp(logits[i][s] - m[i]) for s in blocks] for i in queries]
    result = jnp.zeros((SAMPLE_ROWS, HEAD_DIM), F32)
    for i in queries:
        denom = jnp.sum(p_own[i], axis=1, keepdims=True)
        acc = _dot(p_own[i].astype(BF16), vn)
        for s in blocks:
            denom = denom + jnp.sum(p_sel[i][s], axis=1, keepdims=True)
            values = vbuf[slot, pl.ds((i * MOBA_TOPK + s) * blk, blk), :].astype(BF16)
            acc = acc + _dot(p_sel[i][s].astype(BF16), values)
        result = jnp.where(orow == i, acc / denom, result)
    o_ref[...] = result.astype(o_ref.dtype)


def _moba_sample(page_table, sel_flat, tbl, proj, cache_k, cache_v, layer, n_seq, n_valid):
    page = cache_k.shape[2]
    past_len = page_table.shape[1] * page
    assert past_len % MOBA_BLOCK == 0 and MOBA_BLOCK % page == 0
    buf_rows = n_valid * MOBA_TOPK * MOBA_BLOCK

    def new_rows(off):
        return pl.BlockSpec((SAMPLE_ROWS, HEAD_DIM), lambda b, h, pt, sel: (b, off + h))

    return pl.pallas_call(
        functools.partial(_moba_sample_kernel, layer=layer, n_valid=n_valid, past_len=past_len, page=page),
        grid_spec=pltpu.PrefetchScalarGridSpec(
            num_scalar_prefetch=2,
            grid=(n_seq, N_HEADS),
            in_specs=[pl.BlockSpec(memory_space=pltpu.SMEM),
                      new_rows(QMB_BLK), new_rows(KMB_BLK), new_rows(VMB_BLK),
                      pl.BlockSpec(memory_space=pl.ANY), pl.BlockSpec(memory_space=pl.ANY)],
            out_specs=pl.BlockSpec((SAMPLE_ROWS, HEAD_DIM), lambda b, h, pt, sel: (b, h)),
            scratch_shapes=[pltpu.VMEM((2, buf_rows, HEAD_DIM), F32), pltpu.VMEM((2, buf_rows, HEAD_DIM), F32),
                            pltpu.SemaphoreType.DMA((2, 2))],
        ),
        out_shape=jax.ShapeDtypeStruct((n_seq * SAMPLE_ROWS, GROUP_W), BF16),
        compiler_params=_cparams("arbitrary", "arbitrary"),
        name="moba_sample",
    )(page_table, sel_flat, tbl, proj, proj, proj, cache_k, cache_v)


def _pad_row(v):
    return jnp.pad(v.astype(F32), (0, HEAD_DIM - v.shape[0])).reshape(1, HEAD_DIM)


def kernel(x_prompt, x_sample, cache_k, cache_v, page_table, state_conv, state_delta, w_in, conv_w, a_log, dt_bias,
           gn_w, w_out, norm_mix, norm_ffn, w_gate, w_up, w_down, rel_bias, final_norm):
    n_layers = w_in.shape[0]
    bp, t_len, d_model = x_prompt.shape
    bs, t_dec, _ = x_sample.shape
    conv_ch = 3 * GROUP_W
    a_off = conv_ch + GROUP_W
    mb_off = a_off + 2 * N_HEADS
    assert t_len % MOBA_BLOCK == 0 and t_len % DN_CHUNK == 0 and CONV_W - 1 <= t_dec <= SAMPLE_ROWS
    assert cache_k.shape[3:] == (N_HEADS, HEAD_DIM)

    xp = x_prompt.reshape(bp * t_len, d_model)
    xs = jnp.pad(x_sample, ((0, 0), (0, SAMPLE_ROWS - t_dec), (0, 0))).reshape(bs * SAMPLE_ROWS, d_model)
    tm_p, tm_s = TM_PROMPT, bs * SAMPLE_ROWS

    tbl = rel_bias.astype(F32).T

    conv_prev_p = jnp.zeros((1, bp, SUBLANES, conv_ch), F32)
    conv_prev_s = jnp.pad(state_conv.astype(F32), ((0, 0), (0, 0), (SUBLANES - (CONV_W - 1), 0), (0, 0)))
    s0_p = jnp.zeros((1, bp, N_HEADS, HEAD_DIM, HEAD_DIM), F32)
    s0_s = state_delta.astype(F32)

    w_main, w_ab = _split_w_in(w_in, a_off, mb_off)
    w_out16, wg, wu, wd = w_out.astype(BF16), w_gate.astype(BF16), w_up.astype(BF16), w_down.astype(BF16)

    outs = {name: [] for name in ("kp", "vp", "cp", "sp", "kd", "vd", "cd", "sd")}
    for l in range(n_layers):
        nm, nf = norm_mix[l].reshape(1, d_model), norm_ffn[l].reshape(1, d_model)
        alog_row, dtb_row, gn_row = _pad_row(a_log[l]), _pad_row(dt_bias[l]), gn_w[l].reshape(1, HEAD_DIM)

        proj, gates = _inproj(xp, nm, w_main, w_ab, alog_row, dtb_row, l, tm_p, DN_CHUNK, DN_CHUNK)
        mix_dn, s_new = _deltanet(proj, gates, conv_prev_p, conv_w[l], gn_row, s0_p, l, bp, t_len, t_len)
        mix_mb = _moba_prompt(tbl, proj, bp, t_len)
        xp = _outproj(xp, mix_dn, mix_mb, w_out16, l, tm_p)
        act, kmean = _ffn_up(xp, nf, wg, wu, l, tm_p, cache_k, page_table)
        xp = _ffn_down(xp, act, wd, l, tm_p)
        kmean_t = kmean.transpose(0, 2, 1, 3)
        proj3 = proj.reshape(bp, t_len, MAIN_COLS)
        outs["kp"].append(proj3[:, :, KMB_BLK * HEAD_DIM:VMB_BLK * HEAD_DIM].reshape(bp, t_len, N_HEADS, HEAD_DIM))
        outs["vp"].append(proj3[:, :, VMB_BLK * HEAD_DIM:].reshape(bp, t_len, N_HEADS, HEAD_DIM))
        outs["cp"].append(proj3[:, t_len - (CONV_W - 1):, :conv_ch])
        outs["sp"].append(s_new)

        proj, gates = _inproj(xs, nm, w_main, w_ab, alog_row, dtb_row, l, tm_s, SAMPLE_ROWS, t_dec)
        mix_dn, s_new = _deltanet(proj, gates, conv_prev_s, conv_w[l], gn_row, s0_s, l, bs, SAMPLE_ROWS, t_dec)
        sel = _sample_select(proj, kmean_t, bs)
        sel_flat = sel[:, :, :t_dec, :MOBA_TOPK].reshape(-1)
        mix_mb = _moba_sample(page_table, sel_flat, tbl, proj, cache_k, cache_v, l, bs, t_dec)
        xs = _outproj(xs, mix_dn, mix_mb, w_out16, l, tm_s)
        xs = _ffn_down(xs, _ffn_up(xs, nf, wg, wu, l, tm_s), wd, l, tm_s)
        proj3 = proj.reshape(bs, SAMPLE_ROWS, MAIN_COLS)
        outs["kd"].append(proj3[:, :t_dec, KMB_BLK * HEAD_DIM:VMB_BLK * HEAD_DIM].reshape(bs, t_dec, N_HEADS, HEAD_DIM))
        outs["vd"].append(proj3[:, :t_dec, VMB_BLK * HEAD_DIM:].reshape(bs, t_dec, N_HEADS, HEAD_DIM))
        outs["cd"].append(proj3[:, t_dec - (CONV_W - 1):t_dec, :conv_ch])
        outs["sd"].append(s_new)

    fn = final_norm.reshape(1, d_model)
    y_prompt = _final_norm(xp, fn, tm_p).reshape(bp, t_len, d_model)
    y_sample = _final_norm(xs, fn, tm_s).reshape(bs, SAMPLE_ROWS, d_model)[:, :t_dec]
    return (y_prompt, y_sample) + tuple(jnp.stack(outs[n]) for n in ("kp", "vp", "cp", "sp", "kd", "vd", "cd", "sd"))
```

```python
import functools
import math

import jax
import jax.numpy as jnp
import numpy as np
from jax import lax
from jax.experimental import pallas as pl
from jax.experimental.pallas import tpu as pltpu

F32 = jnp.float32
BF16 = jnp.bfloat16
HI = lax.Precision.HIGHEST

HEAD_DIM = 128
N_HEADS = 8
GROUP_W = N_HEADS * HEAD_DIM
CONV_W = 4
DN_CHUNK = 128
DN_HEADS_PER_STEP = 2
DN_CHAINS = 8
MOBA_BLOCK = 256
MOBA_TOPK = 3
MOBA_HEADS_PER_STEP = 4
REL_BUCKETS = 32
REL_MAX_DIST = 128
RMS_EPS = 1e-6
L2_EPS = 1e-6
SUBLANES = 8
SAMPLE_ROWS = 8

QDN_BLK, KDN_BLK, VDN_BLK, Z_BLK, QMB_BLK, KMB_BLK, VMB_BLK = 0, 8, 16, 24, 32, 40, 48
MAIN_COLS = 56 * HEAD_DIM

TM_PROMPT = 1024
TN_INPROJ = 1024
TN_OUTPROJ = 1024
TN_FFN_UP = 512
TN_FFN_DOWN = 512
PAGES_PER_STEP = 16
TC_W_IN = 512
VMEM_LIMIT = 56 * 1024 * 1024


def _bucket_thresholds():
    d = np.arange(0, 4 * REL_MAX_DIST)
    max_exact = REL_BUCKETS // 2
    nf = np.maximum(d, 1).astype(np.float32)
    large = max_exact + (np.log(nf / np.float32(max_exact)) / np.float32(math.log(REL_MAX_DIST / max_exact))
                         * (REL_BUCKETS - max_exact)).astype(np.int32)
    bucket = np.where(d < max_exact, d, np.minimum(large, REL_BUCKETS - 1))
    return tuple(int(np.argmax(bucket >= b)) for b in range(REL_BUCKETS))


BUCKET_THR = _bucket_thresholds()


def _cparams(*sem):
    return pltpu.CompilerParams(dimension_semantics=sem, vmem_limit_bytes=VMEM_LIMIT)


def _rms_rows(x, w_row):
    return x * lax.rsqrt(jnp.mean(x * x, axis=-1, keepdims=True) + RMS_EPS) * w_row


def _dot(a, b, precision=None):
    return jnp.dot(a, b, preferred_element_type=F32, precision=precision)


def _dot_nt(a, b, precision=None):
    return lax.dot_general(a, b, (((1,), (1,)), ((), ())), preferred_element_type=F32, precision=precision)


def _split_bf16(a):
    hi = a.astype(BF16)
    return hi, (a - hi.astype(F32)).astype(BF16)


def _dot3(a_parts, b_parts):
    (ah, al), (bh, bl) = a_parts, b_parts
    return _dot(ah, bh) + (_dot(ah, bl) + _dot(al, bh))


def _split_w_in_kernel(w_ref, ab_ref, o_ref, oab_ref):
    o_ref[...] = w_ref[0].T.astype(BF16)

    @pl.when(pl.program_id(1) == 0)
    def _():
        ab = ab_ref[0].T
        pad = jnp.zeros((ab.shape[0], HEAD_DIM - ab.shape[1]), F32)
        oab_ref[...] = jnp.concatenate([ab, pad], axis=1).astype(BF16)


def _split_w_in(w_in, a_off, mb_off):
    n_layers, d, cols = w_in.shape
    tc = TC_W_IN
    assert a_off + (cols - mb_off) == MAIN_COLS and a_off % tc == 0 and MAIN_COLS % tc == 0
    assert a_off % SUBLANES == 0 and mb_off % SUBLANES == 0
    w_t = jnp.swapaxes(w_in, 1, 2)

    def rows(l, c):
        return (l, pl.multiple_of(c * tc + jnp.where(c * tc >= a_off, mb_off - a_off, 0), SUBLANES), 0)

    return pl.pallas_call(
        _split_w_in_kernel,
        grid=(n_layers, MAIN_COLS // tc),
        in_specs=[pl.BlockSpec((pl.Element(1), pl.Element(tc), pl.Element(d)), rows),
                  pl.BlockSpec((pl.Element(1), pl.Element(mb_off - a_off), pl.Element(d)), lambda l, c: (l, a_off, 0))],
        out_specs=[pl.BlockSpec((None, d, tc), lambda l, c: (l, 0, c)),
                   pl.BlockSpec((None, d, HEAD_DIM), lambda l, c: (l, 0, 0))],
        out_shape=[jax.ShapeDtypeStruct((n_layers, d, MAIN_COLS), BF16),
                   jax.ShapeDtypeStruct((n_layers, d, HEAD_DIM), BF16)],
        compiler_params=_cparams("arbitrary", "arbitrary"),
        name="split_w_in",
    )(w_t, w_t)


def _inproj_kernel(x_ref, nw_ref, w_ref, wab_ref, alog_ref, dtb_ref, o_ref, ogb_ref, h_scr, *, seg, t_valid):
    @pl.when(pl.program_id(1) == 0)
    def _():
        hb = _rms_rows(x_ref[...], nw_ref[...]).astype(BF16)
        h_scr[...] = hb
        ab = _dot(hb, wab_ref[...])
        tm = ab.shape[0]
        ch = min(tm, DN_CHUNK)
        z_ab = ab + dtb_ref[...]
        softplus = jnp.maximum(z_ab, 0.0) + jnp.log1p(jnp.exp(-jnp.abs(z_ab)))
        g = -jnp.exp(alog_ref[...]) * softplus
        beta = jax.nn.sigmoid(ab)
        if t_valid < seg:
            valid = (lax.broadcasted_iota(jnp.int32, ab.shape, 0) & (seg - 1)) < t_valid
            g, beta = jnp.where(valid, g, 0.0), jnp.where(valid, beta, 0.0)
        ii = lax.broadcasted_iota(jnp.int32, (ch, ch), 0)
        jj = lax.broadcasted_iota(jnp.int32, (ch, ch), 1)
        same_run_before = ((ii >= jj) & ((ii - jj) <= (ii & (seg - 1)))).astype(F32)
        is_g = lax.broadcasted_iota(jnp.int32, (ch, HEAD_DIM), 1) < N_HEADS
        for c0 in range(0, tm, ch):
            gc = _dot(same_run_before, g[c0:c0 + ch], HI)
            ogb_ref[c0:c0 + ch, :] = jnp.where(is_g, gc, beta[c0:c0 + ch])

    o_ref[...] = _dot(h_scr[...], w_ref[...])


def _inproj(x, nw, w_main, w_ab, alog_row, dtb_row, layer, tm, seg, t_valid):
    m, d = x.shape
    n = w_main.shape[2]
    tn = TN_INPROJ
    assert seg & (seg - 1) == 0 and min(tm, DN_CHUNK) % seg == 0 and tm % min(tm, DN_CHUNK) == 0
    row_spec = pl.BlockSpec((1, HEAD_DIM), lambda i, j: (0, 0))
    return pl.pallas_call(
        functools.partial(_inproj_kernel, seg=seg, t_valid=t_valid),
        grid=(m // tm, n // tn),
        in_specs=[pl.BlockSpec((tm, d), lambda i, j: (i, 0)),
                  pl.BlockSpec((1, d), lambda i, j: (0, 0)),
                  pl.BlockSpec((None, d, tn), lambda i, j: (layer, 0, j)),
                  pl.BlockSpec((None, d, HEAD_DIM), lambda i, j: (layer, 0, 0)),
                  row_spec, row_spec],
        out_specs=[pl.BlockSpec((tm, tn), lambda i, j: (i, j)),
                   pl.BlockSpec((tm, HEAD_DIM), lambda i, j: (i, 0))],
        out_shape=[jax.ShapeDtypeStruct((m, n), F32), jax.ShapeDtypeStruct((m, HEAD_DIM), F32)],
        scratch_shapes=[pltpu.VMEM((tm, d), BF16)],
        compiler_params=_cparams("parallel", "arbitrary"),
        name="inproj",
    )(x, nw, w_main, w_ab, alog_row, dtb_row)


def _outproj_kernel(x_ref, a_ref, b_ref, wa_ref, wb_ref, o_ref):
    o_ref[...] = x_ref[...] + (_dot(a_ref[...], wa_ref[...]) + _dot(b_ref[...], wb_ref[...]))


def _outproj(x, mix_dn, mix_mb, w_out, layer, tm):
    m, d = x.shape
    k = mix_dn.shape[1]
    tn = TN_OUTPROJ
    return pl.pallas_call(
        _outproj_kernel,
        grid=(m // tm, d // tn),
        in_specs=[pl.BlockSpec((tm, tn), lambda i, j: (i, j)),
                  pl.BlockSpec((tm, k), lambda i, j: (i, 0)),
                  pl.BlockSpec((tm, k), lambda i, j: (i, 0)),
                  pl.BlockSpec((None, k, tn), lambda i, j: (layer, 0, j)),
                  pl.BlockSpec((None, k, tn), lambda i, j: (layer, 1, j))],
        out_specs=pl.BlockSpec((tm, tn), lambda i, j: (i, j)),
        out_shape=jax.ShapeDtypeStruct((m, d), F32),
        compiler_params=_cparams("parallel", "arbitrary"),
        name="outproj",
    )(x, mix_dn, mix_mb, w_out, w_out)


def _ffn_up_kernel(*refs, n_mean_steps, n_col_steps):
    if n_mean_steps:
        refs = refs[1:]
    x_ref, nw_ref, wg_ref, wu_ref = refs[:4]
    if n_mean_steps:
        pages = refs[4:4 + PAGES_PER_STEP]
        o_ref, km_ref, h_scr = refs[4 + PAGES_PER_STEP:]
    else:
        o_ref, h_scr = refs[4:]

    @pl.when(pl.program_id(1) == 0)
    def _():
        h_scr[...] = _rms_rows(x_ref[...], nw_ref[...]).astype(BF16)

    h = h_scr[...]
    g = _dot(h, wg_ref[...])
    u = _dot(h, wu_ref[...])
    o_ref[...] = (g * jax.nn.sigmoid(g) * u).astype(BF16)

    if n_mean_steps:
        @pl.when(pl.program_id(0) * n_col_steps + pl.program_id(1) < n_mean_steps)
        def _():
            pages_per_block = PAGES_PER_STEP // SUBLANES
            for n in range(SUBLANES):
                tot = jnp.sum(pages[n * pages_per_block][...], axis=0)
                for p in range(1, pages_per_block):
                    tot = tot + jnp.sum(pages[n * pages_per_block + p][...], axis=0)
                km_ref[n] = tot * (1.0 / MOBA_BLOCK)


def _ffn_up(x, nw, wg, wu, layer, tm, cache_k=None, page_table=None):
    m, d = x.shape
    n = wg.shape[2]
    tn = TN_FFN_UP
    grid = (m // tm, n // tn)
    in_specs = [pl.BlockSpec((tm, d), lambda i, j, *_: (i, 0)),
                pl.BlockSpec((1, d), lambda i, j, *_: (0, 0)),
                pl.BlockSpec((None, d, tn), lambda i, j, *_: (layer, 0, j)),
                pl.BlockSpec((None, d, tn), lambda i, j, *_: (layer, 0, j))]
    out_specs = [pl.BlockSpec((tm, tn), lambda i, j, *_: (i, j))]
    out_shape = [jax.ShapeDtypeStruct((m, n), BF16)]
    operands = [x, nw, wg, wu]
    n_mean_steps = 0
    if cache_k is not None:
        _, _, page, n_heads, hd = cache_k.shape
        n_seq, n_pages = page_table.shape
        assert MOBA_BLOCK % page == 0 and PAGES_PER_STEP * page == SUBLANES * MOBA_BLOCK
        steps_per_seq = n_pages // PAGES_PER_STEP
        n_mean_steps = n_seq * steps_per_seq
        assert n_pages % PAGES_PER_STEP == 0 and n_mean_steps <= grid[0] * grid[1]

        def mean_step(i, j):
            ks = jnp.minimum(i * grid[1] + j, n_mean_steps - 1)
            return ks // steps_per_seq, ks % steps_per_seq

        def page_spec(p):
            def index(i, j, pt):
                b, nb = mean_step(i, j)
                return (layer, pt[b, nb * PAGES_PER_STEP + p], 0, 0, 0)
            return pl.BlockSpec((None, None, page, n_heads, hd), index)

        in_specs += [page_spec(p) for p in range(PAGES_PER_STEP)]
        out_specs.append(pl.BlockSpec((None, SUBLANES, n_heads, hd), lambda i, j, pt: (*mean_step(i, j), 0, 0)))
        out_shape.append(jax.ShapeDtypeStruct((n_seq, n_pages * page // MOBA_BLOCK, n_heads, hd), F32))
        operands = [page_table] + operands + [cache_k] * PAGES_PER_STEP
    res = pl.pallas_call(
        functools.partial(_ffn_up_kernel, n_mean_steps=n_mean_steps, n_col_steps=grid[1]),
        grid_spec=pltpu.PrefetchScalarGridSpec(
            num_scalar_prefetch=1 if n_mean_steps else 0, grid=grid, in_specs=in_specs, out_specs=out_specs,
            scratch_shapes=[pltpu.VMEM((tm, d), BF16)]),
        out_shape=out_shape,
        compiler_params=_cparams("arbitrary", "arbitrary"),
        name="ffn_up",
    )(*operands)
    return res if n_mean_steps else res[0]


def _ffn_down_kernel(x_ref, a_ref, w_ref, o_ref):
    o_ref[...] = x_ref[...] + _dot(a_ref[...], w_ref[...])


def _ffn_down(x, act, wd, layer, tm):
    m, d = x.shape
    k = act.shape[1]
    tn = TN_FFN_DOWN
    return pl.pallas_call(
        _ffn_down_kernel,
        grid=(m // tm, d // tn),
        in_specs=[pl.BlockSpec((tm, tn), lambda i, j: (i, j)),
                  pl.BlockSpec((tm, k), lambda i, j: (i, 0)),
                  pl.BlockSpec((None, k, tn), lambda i, j: (layer, 0, j))],
        out_specs=pl.BlockSpec((tm, tn), lambda i, j: (i, j)),
        out_shape=jax.ShapeDtypeStruct((m, d), F32),
        compiler_params=_cparams("parallel", "arbitrary"),
        name="ffn_down",
    )(x, act, wd)


def _final_norm_kernel(x_ref, nw_ref, o_ref):
    o_ref[...] = _rms_rows(x_ref[...], nw_ref[...])


def _final_norm(x, nw, tm):
    m, d = x.shape
    return pl.pallas_call(
        _final_norm_kernel,
        grid=(m // tm,),
        in_specs=[pl.BlockSpec((tm, d), lambda i: (i, 0)), pl.BlockSpec((1, d), lambda i: (0, 0))],
        out_specs=pl.BlockSpec((tm, d), lambda i: (i, 0)),
        out_shape=jax.ShapeDtypeStruct((m, d), F32),
        compiler_params=_cparams("parallel"),
        name="final_norm",
    )(x, nw)


def _deltanet_kernel(q_ref, k_ref, v_ref, z_ref, gb_ref, pq_ref, pk_ref, pv_ref, cq_ref, ck_ref, cv_ref,
                     gn_ref, s0_ref, o_ref, s_ref,
                     u_scr, wq_scr, kdt_scr, attn_scr, egl_scr, *, t_rows, t_valid):
    c = DN_CHUNK
    hd = HEAD_DIM
    heads = range(DN_HEADS_PER_STEP)
    head0 = pl.program_id(1) * DN_HEADS_PER_STEP
    n_chunks = max(t_rows // c, 1)
    rows = min(t_rows, c)
    n_doublings = max((min(c, t_valid) - 1).bit_length() - 1, 0)
    group = math.gcd(n_chunks, max(DN_CHAINS // DN_HEADS_PER_STEP, 1))

    ii = lax.broadcasted_iota(jnp.int32, (c, c), 0)
    jj = lax.broadcasted_iota(jnp.int32, (c, c), 1)
    tril = ii >= jj
    strict = ii > jj
    eye_f = (ii == jj).astype(F32)
    ones_cc = jnp.ones((c, c), F32)
    lane = lax.broadcasted_iota(jnp.int32, (rows, hd), 1)
    gn = gn_ref[...]

    def each(fn, *lists):
        return [fn(*args) for args in zip(*lists)]

    def cols(hh):
        return slice(hh * hd, (hh + 1) * hd)

    def load(ref, r0, hh):
        x = ref[pl.ds(r0, rows), cols(hh)]
        if rows < c:
            x = jnp.concatenate([x, jnp.zeros((c - rows, hd), F32)], axis=0)
        return x

    def conv_silu(ref, prev_ref, taps_ref, ci, hh):
        r0 = pl.multiple_of(ci * rows, rows)
        x = load(ref, r0, hh)
        tail = prev_ref[:, cols(hh)]
        if n_chunks > 1:
            before = ref[pl.ds(pl.multiple_of(jnp.maximum(r0 - SUBLANES, 0), SUBLANES), SUBLANES), cols(hh)]
            tail = jnp.where(ci == 0, tail, before)
        win = jnp.concatenate([tail, x], axis=0)
        taps = taps_ref[:, cols(hh)]
        y = win[5:5 + c] * taps[0:1]
        for i in range(1, CONV_W):
            y = y + win[5 + i:5 + i + c] * taps[i:i + 1]
        return y * jax.nn.sigmoid(y)

    def l2n(x):
        return x * lax.rsqrt(jnp.sum(x * x, axis=-1, keepdims=True) + L2_EPS)

    def gates(ci, hh):
        gb = gb_ref[pl.ds(pl.multiple_of(ci * rows, rows), rows), :]
        gc = jnp.sum(jnp.where(lane == head0 + hh, gb, 0.0), axis=1, keepdims=True)
        beta = jnp.sum(jnp.where(lane == N_HEADS + head0 + hh, gb, 0.0), axis=1, keepdims=True)
        if rows < c:
            gc = jnp.concatenate([gc, jnp.broadcast_to(gc[rows - 1:rows], (c - rows, 1))], axis=0)
            beta = jnp.concatenate([beta, jnp.zeros((c - rows, 1), F32)], axis=0)
        return gc, beta

    def prepare(gi, carry):
        cis = [gi * group + t for t in range(group) for _ in heads]
        hhs = [hh for _ in range(group) for hh in heads]
        q = each(lambda ci, hh: l2n(conv_silu(q_ref, pq_ref, cq_ref, ci, hh)) * (hd ** -0.5), cis, hhs)
        k = each(lambda ci, hh: l2n(conv_silu(k_ref, pk_ref, ck_ref, ci, hh)), cis, hhs)
        v = each(lambda ci, hh: conv_silu(v_ref, pv_ref, cv_ref, ci, hh), cis, hhs)
        gc, beta = zip(*each(gates, cis, hhs))
        gc_row = each(lambda g: _dot(ones_cc, eye_f * g, HI), gc)
        decay = each(lambda g, gr: jnp.where(tril, jnp.exp(jnp.where(tril, g - gr, 0.0)), 0.0), gc, gc_row)
        e_gc = each(jnp.exp, gc)
        g_last = each(lambda g: g[c - 1:c, :], gc)
        kb = each(lambda a, bt: a * bt, k, beta)
        vb = each(lambda a, bt: a * bt, v, beta)
        k16 = each(lambda a: a.astype(BF16), k)
        neg_l = each(lambda a, b16, dc: jnp.where(strict, -(_dot_nt(a.astype(BF16), b16) * dc), 0.0), kb, k16, decay)
        tinv = each(lambda m: eye_f + m, neg_l)
        p = each(_split_bf16, neg_l)
        for _ in range(n_doublings):
            p = each(lambda pp: _split_bf16(_dot3(pp, pp)), p)
            tinv = each(lambda t, pp: t + _dot3(_split_bf16(t), pp), tinv, p)
        uw = each(lambda t, a, b, e: _dot3(_split_bf16(t), _split_bf16(jnp.concatenate([a, b * e], axis=1))),
                  tinv, vb, kb, e_gc)
        attn = each(lambda a, b16, dc: jnp.where(tril, _dot_nt(a.astype(BF16), b16) * dc, 0.0).astype(BF16),
                    q, k16, decay)
        kdt = each(lambda a, gl, g: (a * jnp.exp(gl - g)).T.astype(BF16), k, g_last, gc)
        for n, (ci, hh) in enumerate(zip(cis, hhs)):
            c0 = pl.multiple_of(ci * c, c)
            u_scr[hh, pl.ds(c0, c), :] = uw[n][:, :hd]
            wq_scr[hh, pl.ds(2 * c0, c), :] = uw[n][:, hd:].astype(BF16)
            wq_scr[hh, pl.ds(2 * c0 + c, c), :] = (q[n] * e_gc[n]).astype(BF16)
            attn_scr[hh, pl.ds(c0, c), :] = attn[n]
            kdt_scr[hh, pl.ds(c0, c), :] = kdt[n]
            egl_scr[hh, pl.ds(pl.multiple_of(ci * SUBLANES, SUBLANES), SUBLANES), :] = jnp.broadcast_to(
                jnp.exp(g_last[n]), (SUBLANES, hd))
        return carry

    lax.fori_loop(0, n_chunks // group, prepare, 0)

    def recur(ci, states):
        r0 = pl.multiple_of(ci * rows, rows)
        c0 = pl.multiple_of(ci * c, c)
        e0 = pl.multiple_of(ci * SUBLANES, SUBLANES)
        r = each(lambda hh, s: _dot(wq_scr[hh, pl.ds(2 * c0, 2 * c), :], s.astype(BF16)), heads, states)
        v_new = each(lambda hh, rr: (u_scr[hh, pl.ds(c0, c), :] - rr[:c]).astype(BF16), heads, r)
        o = each(lambda hh, rr, vn: rr[c:] + _dot(attn_scr[hh, pl.ds(c0, c), :], vn), heads, r, v_new)
        new_states = each(lambda hh, s, vn: s * egl_scr[hh, pl.ds(e0, SUBLANES), :][0:1]
                          + _dot(kdt_scr[hh, pl.ds(c0, c), :], vn), heads, states, v_new)
        for hh in heads:
            on = o[hh] * lax.rsqrt(jnp.mean(o[hh] * o[hh], axis=-1, keepdims=True) + RMS_EPS) * gn
            zc = load(z_ref, r0, hh)
            out = (on * (zc * jax.nn.sigmoid(zc))).astype(o_ref.dtype)
            o_ref[pl.ds(r0, rows), cols(hh)] = out[:rows]
        return tuple(new_states)

    final = lax.fori_loop(0, n_chunks, recur, tuple(s0_ref[hh] for hh in heads))
    for hh in heads:
        s_ref[hh] = final[hh]


def _deltanet(proj, gates, conv_prev, conv_w, gn_row, s0, layer, n_seq, t_rows, t_valid):
    lp = layer if conv_prev.shape[0] > 1 else 0
    t_pad = max(t_rows // DN_CHUNK, 1) * DN_CHUNK
    n_chunks = t_pad // DN_CHUNK
    hps = DN_HEADS_PER_STEP
    width = hps * HEAD_DIM
    assert N_HEADS % hps == 0

    def col(off):
        return pl.BlockSpec((t_rows, width), lambda b, h: (b, off // hps + h))

    def prev(off):
        return pl.BlockSpec((None, None, SUBLANES, width), lambda b, h: (lp, b, 0, off // hps + h))

    def taps(off):
        return pl.BlockSpec((CONV_W, width), lambda b, h: (0, off // hps + h))

    row_spec = pl.BlockSpec((1, HEAD_DIM), lambda b, h: (0, 0))
    return pl.pallas_call(
        functools.partial(_deltanet_kernel, t_rows=t_rows, t_valid=t_valid),
        grid=(n_seq, N_HEADS // hps),
        in_specs=[col(QDN_BLK), col(KDN_BLK), col(VDN_BLK), col(Z_BLK),
                  pl.BlockSpec((t_rows, HEAD_DIM), lambda b, h: (b, 0)),
                  prev(QDN_BLK), prev(KDN_BLK), prev(VDN_BLK),
                  taps(QDN_BLK), taps(KDN_BLK), taps(VDN_BLK),
                  row_spec,
                  pl.BlockSpec((None, None, hps, HEAD_DIM, HEAD_DIM), lambda b, h: (lp, b, h, 0, 0))],
        out_specs=[pl.BlockSpec((t_rows, width), lambda b, h: (b, h)),
                   pl.BlockSpec((None, hps, HEAD_DIM, HEAD_DIM), lambda b, h: (b, h, 0, 0))],
        out_shape=[jax.ShapeDtypeStruct((n_seq * t_rows, GROUP_W), BF16),
                   jax.ShapeDtypeStruct((n_seq, N_HEADS, HEAD_DIM, HEAD_DIM), F32)],
        scratch_shapes=[pltpu.VMEM((hps, t_pad, HEAD_DIM), F32),
                        pltpu.VMEM((hps, 2 * t_pad, HEAD_DIM), BF16),
                        pltpu.VMEM((hps, t_pad, DN_CHUNK), BF16),
                        pltpu.VMEM((hps, t_pad, DN_CHUNK), BF16),
                        pltpu.VMEM((hps, n_chunks * SUBLANES, HEAD_DIM), F32)],
        compiler_params=_cparams("parallel", "arbitrary"),
        name="deltanet",
    )(proj, proj, proj, proj, gates, conv_prev, conv_prev, conv_prev, conv_w, conv_w, conv_w, gn_row, s0)


def _rel_bias_tile(tbl_ref, h, dist):
    val = jnp.full(dist.shape, tbl_ref[h, 0], F32)
    for b in range(1, REL_BUCKETS):
        val = jnp.where(dist >= BUCKET_THR[b], tbl_ref[h, b], val)
    return val


def _gate_rank(gate, n_cand, idx, axis):
    rank = jnp.zeros(gate.shape, F32)
    for m in range(n_cand):
        gm = lax.slice_in_dim(gate, m, m + 1, axis=axis)
        ahead = (gm > gate) | ((gm == gate) & (m < idx))
        rank = rank + ahead.astype(F32)
    return rank


def _moba_prompt_kernel(tbl_ref, q_ref, k_ref, v_ref, o_ref, kmean_scr, bown_scr, bnear_scr, k16_scr, vt16_scr,
                        *, n_blocks):
    blk = MOBA_BLOCK
    hd = HEAD_DIM
    heads = range(MOBA_HEADS_PER_STEP)
    head0 = pl.program_id(0) * MOBA_HEADS_PER_STEP
    b = pl.program_id(1)
    j = pl.program_id(2)
    scale = hd ** -0.5

    def each(fn, *lists):
        return [fn(*args) for args in zip(*lists)]

    def cols(hh):
        return slice(hh * hd, (hh + 1) * hd)

    @pl.when((b == 0) & (j == 0))
    def _():
        ki = lax.broadcasted_iota(jnp.int32, (blk, blk), 0)
        qi = lax.broadcasted_iota(jnp.int32, (blk, blk), 1)
        d = qi - ki
        for hh in heads:
            bown_scr[hh] = jnp.where(d >= 0, _rel_bias_tile(tbl_ref, head0 + hh, d), -jnp.inf)
            bnear_scr[hh] = _rel_bias_tile(tbl_ref, head0 + hh, d + blk)

    @pl.when(j == 0)
    def _():
        for hh in heads:
            kmean_scr[hh] = jnp.zeros(kmean_scr.shape[1:], F32)
            for n in range(n_blocks):
                kmean_scr[hh, n:n + 1, :] = jnp.sum(k_ref[n * blk:(n + 1) * blk, cols(hh)], axis=0,
                                                    keepdims=True) * (1.0 / blk)
            k16_scr[hh] = k_ref[:, cols(hh)].astype(BF16)
            for n in range(n_blocks * blk // hd):
                rows = slice(n * hd, (n + 1) * hd)
                vt16_scr[hh, :, rows] = v_ref[rows, cols(hh)].T.astype(BF16)

    far_bias = [tbl_ref[head0 + hh, REL_BUCKETS - 1] for hh in heads]

    def attend(jb):
        q = [q_ref[:, cols(hh)] for hh in heads]
        q16 = each(lambda a: a.astype(BF16), q)
        n_keys = (jb + 1) * blk
        gated = jb > MOBA_TOPK
        if gated:
            cand = lax.broadcasted_iota(jnp.int32, (SUBLANES, blk), 0)
            gate_t = each(lambda hh, a: jnp.where(cand < jb, _dot_nt(kmean_scr[hh], a, HI)[:SUBLANES], -jnp.inf),
                          heads, q)
            keep_t = each(lambda g: (_gate_rank(g, jb, cand, 0) < MOBA_TOPK).astype(F32), gate_t)
        pieces = [[] for _ in heads]
        for n in range(jb + 1):
            for hh in heads:
                sn = _dot_nt(k16_scr[hh, n * blk:(n + 1) * blk, :], q16[hh]) * scale
                if n == jb:
                    sn = sn + bown_scr[hh]
                else:
                    sn = sn + (bnear_scr[hh] if n == jb - 1 else far_bias[hh])
                    if gated:
                        sn = jnp.where(keep_t[hh][n:n + 1, :] > 0.0, sn, -jnp.inf)
                pieces[hh].append(sn)

        def col_max(ps):
            m = jnp.max(ps[0], axis=0, keepdims=True)
            for sn in ps[1:]:
                m = jnp.maximum(m, jnp.max(sn, axis=0, keepdims=True))
            return m

        def col_sum(ps):
            tot = jnp.sum(ps[0], axis=0, keepdims=True)
            for pn in ps[1:]:
                tot = tot + jnp.sum(pn, axis=0, keepdims=True)
            return tot

        m = each(col_max, pieces)
        probs = each(lambda ps, mm: [jnp.exp(sn - mm) for sn in ps], pieces, m)
        denom = each(col_sum, probs)
        p = each(lambda ps: jnp.concatenate([pn.astype(BF16) for pn in ps], axis=0) if jb else ps[0].astype(BF16),
                 probs)
        o_t = each(lambda hh, pp, dn: _dot(vt16_scr[hh, :, 0:n_keys], pp) / dn, heads, p, denom)
        for hh in heads:
            o_ref[:, cols(hh)] = o_t[hh].T.astype(o_ref.dtype)

    for jb in range(n_blocks):
        pl.when(j == jb)(functools.partial(attend, jb))


def _moba_prompt(tbl, proj, n_seq, t_len):
    blk = MOBA_BLOCK
    n_blocks = t_len // blk
    hps = MOBA_HEADS_PER_STEP
    width = hps * HEAD_DIM
    assert MOBA_BLOCK >= BUCKET_THR[-1] and n_blocks <= SUBLANES and N_HEADS % hps == 0
    return pl.pallas_call(
        functools.partial(_moba_prompt_kernel, n_blocks=n_blocks),
        grid=(N_HEADS // hps, n_seq, n_blocks),
        in_specs=[pl.BlockSpec(memory_space=pltpu.SMEM),
                  pl.BlockSpec((blk, width), lambda h, b, j: (b * n_blocks + j, QMB_BLK // hps + h)),
                  pl.BlockSpec((t_len, width), lambda h, b, j: (b, KMB_BLK // hps + h)),
                  pl.BlockSpec((t_len, width), lambda h, b, j: (b, VMB_BLK // hps + h))],
        out_specs=pl.BlockSpec((blk, width), lambda h, b, j: (b * n_blocks + j, h)),
        out_shape=jax.ShapeDtypeStruct((n_seq * t_len, GROUP_W), BF16),
        scratch_shapes=[pltpu.VMEM((hps, HEAD_DIM, HEAD_DIM), F32),
                        pltpu.VMEM((hps, blk, blk), F32), pltpu.VMEM((hps, blk, blk), F32),
                        pltpu.VMEM((hps, t_len, HEAD_DIM), BF16), pltpu.VMEM((hps, HEAD_DIM, t_len), BF16)],
        compiler_params=_cparams("arbitrary", "arbitrary", "arbitrary"),
        name="moba_prompt",
    )(tbl, proj, proj, proj)


def _sample_select_kernel(q_ref, km_ref, o_ref, *, n_past_blocks):
    lane = lax.broadcasted_iota(jnp.int32, (SAMPLE_ROWS, HEAD_DIM), 1)
    lane_f = lane.astype(F32)
    pad = jnp.zeros((HEAD_DIM - n_past_blocks, HEAD_DIM), F32)
    gates = [jnp.where(lane < n_past_blocks,
                       _dot_nt(q_ref[:, hh * HEAD_DIM:(hh + 1) * HEAD_DIM],
                               jnp.concatenate([km_ref[hh], pad], axis=0), HI), -jnp.inf)
             for hh in range(N_HEADS)]
    for hh, gate in enumerate(gates):
        rank = _gate_rank(gate, n_past_blocks, lane, 1)
        out = jnp.zeros((SAMPLE_ROWS, HEAD_DIM), F32)
        for r in range(MOBA_TOPK):
            idx = jnp.sum(jnp.where(rank == r, lane_f, 0.0), axis=1, keepdims=True)
            out = jnp.where(lane == r, idx, out)
        o_ref[hh] = out.astype(jnp.int32)


def _sample_select(proj, kmean_t, n_seq):
    n_past_blocks = kmean_t.shape[2]
    assert MOBA_TOPK <= n_past_blocks <= HEAD_DIM
    return pl.pallas_call(
        functools.partial(_sample_select_kernel, n_past_blocks=n_past_blocks),
        grid=(n_seq,),
        in_specs=[pl.BlockSpec((SAMPLE_ROWS, GROUP_W), lambda b: (b, QMB_BLK // N_HEADS)),
                  pl.BlockSpec((None, N_HEADS, n_past_blocks, HEAD_DIM), lambda b: (b, 0, 0, 0))],
        out_specs=pl.BlockSpec((None, N_HEADS, SAMPLE_ROWS, HEAD_DIM), lambda b: (b, 0, 0, 0)),
        out_shape=jax.ShapeDtypeStruct((n_seq, N_HEADS, SAMPLE_ROWS, HEAD_DIM), jnp.int32),
        compiler_params=_cparams("arbitrary"),
        name="sample_select",
    )(proj, kmean_t)


def _moba_sample_kernel(pt_ref, sel_ref, tbl_ref, q_ref, kn_ref, vn_ref, ck_hbm, cv_hbm, o_ref, kbuf, vbuf, sem,
                        *, layer, n_valid, past_len, page):
    blk = MOBA_BLOCK
    pages_per_block = blk // page
    b, h = pl.program_id(0), pl.program_id(1)
    step = b * N_HEADS + h
    n_steps = pl.num_programs(0) * N_HEADS
    slot = step % 2
    scale = HEAD_DIM ** -0.5

    def page_copies(bb, hh, sl):
        copies = []
        for i in range(n_valid):
            for s in range(MOBA_TOPK):
                blk_idx = sel_ref[((bb * N_HEADS + hh) * n_valid + i) * MOBA_TOPK + s]
                for p in range(pages_per_block):
                    pg = pt_ref[bb, blk_idx * pages_per_block + p]
                    dst_rows = pl.ds(((i * MOBA_TOPK + s) * pages_per_block + p) * page, page)
                    copies.append(pltpu.make_async_copy(ck_hbm.at[layer, pg, :, hh, :],
                                                        kbuf.at[sl, dst_rows, :], sem.at[0, sl]))
                    copies.append(pltpu.make_async_copy(cv_hbm.at[layer, pg, :, hh, :],
                                                        vbuf.at[sl, dst_rows, :], sem.at[1, sl]))
        return copies

    @pl.when(step == 0)
    def _():
        for cp in page_copies(b, h, slot):
            cp.start()

    @pl.when(step + 1 < n_steps)
    def _():
        nxt = step + 1
        for cp in page_copies(nxt // N_HEADS, nxt % N_HEADS, 1 - slot):
            cp.start()

    for cp in page_copies(b, h, slot):
        cp.wait()

    qb = q_ref[...].astype(BF16)
    qrow = lax.broadcasted_iota(jnp.int32, (SAMPLE_ROWS, blk), 0)
    koff = lax.broadcasted_iota(jnp.int32, (SAMPLE_ROWS, blk), 1)
    orow = lax.broadcasted_iota(jnp.int32, (SAMPLE_ROWS, HEAD_DIM), 0)

    pad = jnp.zeros((HEAD_DIM - SAMPLE_ROWS, HEAD_DIM), F32)
    kn = jnp.concatenate([kn_ref[...], pad], axis=0).astype(BF16)
    vn = jnp.concatenate([vn_ref[...], pad], axis=0).astype(BF16)
    d_own = qrow[:, :HEAD_DIM] - koff[:, :HEAD_DIM]
    l_own = _dot_nt(qb, kn) * scale + _rel_bias_tile(tbl_ref, h, d_own)
    l_own = jnp.where(d_own >= 0, l_own, -jnp.inf)
    m_own = jnp.max(l_own, axis=1, keepdims=True)

    queries = range(n_valid)
    blocks = range(MOBA_TOPK)

    def score(i, s):
        sel = sel_ref[((b * N_HEADS + h) * n_valid + i) * MOBA_TOPK + s]
        dist = (past_len + qrow) - (sel * blk + koff)
        keys = kbuf[slot, pl.ds((i * MOBA_TOPK + s) * blk, blk), :].astype(BF16)
        return _dot_nt(qb, keys) * scale + _rel_bias_tile(tbl_ref, h, dist)

    logits = [[score(i, s) for s in blocks] for i in queries]
    m = []
    for i in queries:
        mi = m_own
        for l_s in logits[i]:
            mi = jnp.maximum(mi, jnp.max(l_s, axis=1, keepdims=True))
        m.append(mi)
    p_own = [jnp.exp(l_own - m[i]) for i in queries]
    p_sel = [[jnp.exp(logits[i][s] - m[i]) for s in blocks] for i in queries]
    result = jnp.zeros((SAMPLE_ROWS, HEAD_DIM), F32)
    for i in queries:
        denom = jnp.sum(p_own[i], axis=1, keepdims=True)
        acc = _dot(p_own[i].astype(BF16), vn)
        for s in blocks:
            denom = denom + jnp.sum(p_sel[i][s], axis=1, keepdims=True)
            values = vbuf[slot, pl.ds((i * MOBA_TOPK + s) * blk, blk), :].astype(BF16)
            acc = acc + _dot(p_sel[i][s].astype(BF16), values)
        result = jnp.where(orow == i, acc / denom, result)
    o_ref[...] = result.astype(o_ref.dtype)


def _moba_sample(page_table, sel_flat, tbl, proj, cache_k, cache_v, layer, n_seq, n_valid):
    page = cache_k.shape[2]
    past_len = page_table.shape[1] * page
    assert past_len % MOBA_BLOCK == 0 and MOBA_BLOCK % page == 0
    buf_rows = n_valid * MOBA_TOPK * MOBA_BLOCK

    def new_rows(off):
        return pl.BlockSpec((SAMPLE_ROWS, HEAD_DIM), lambda b, h, pt, sel: (b, off + h))

    return pl.pallas_call(
        functools.partial(_moba_sample_kernel, layer=layer, n_valid=n_valid, past_len=past_len, page=page),
        grid_spec=pltpu.PrefetchScalarGridSpec(
            num_scalar_prefetch=2,
            grid=(n_seq, N_HEADS),
            in_specs=[pl.BlockSpec(memory_space=pltpu.SMEM),
                      new_rows(QMB_BLK), new_rows(KMB_BLK), new_rows(VMB_BLK),
                      pl.BlockSpec(memory_space=pl.ANY), pl.BlockSpec(memory_space=pl.ANY)],
            out_specs=pl.BlockSpec((SAMPLE_ROWS, HEAD_DIM), lambda b, h, pt, sel: (b, h)),
            scratch_shapes=[pltpu.VMEM((2, buf_rows, HEAD_DIM), F32), pltpu.VMEM((2, buf_rows, HEAD_DIM), F32),
                            pltpu.SemaphoreType.DMA((2, 2))],
        ),
        out_shape=jax.ShapeDtypeStruct((n_seq * SAMPLE_ROWS, GROUP_W), BF16),
        compiler_params=_cparams("arbitrary", "arbitrary"),
        name="moba_sample",
    )(page_table, sel_flat, tbl, proj, proj, proj, cache_k, cache_v)


def _pad_row(v):
    return jnp.pad(v.astype(F32), (0, HEAD_DIM - v.shape[0])).reshape(1, HEAD_DIM)


def kernel(x_prompt, x_sample, cache_k, cache_v, page_table, state_conv, state_delta, w_in, conv_w, a_log, dt_bias,
           gn_w, w_out, norm_mix, norm_ffn, w_gate, w_up, w_down, rel_bias, final_norm):
    n_layers = w_in.shape[0]
    bp, t_len, d_model = x_prompt.shape
    bs, t_dec, _ = x_sample.shape
    conv_ch = 3 * GROUP_W
    a_off = conv_ch + GROUP_W
    mb_off = a_off + 2 * N_HEADS
    assert t_len % MOBA_BLOCK == 0 and t_len % DN_CHUNK == 0 and CONV_W - 1 <= t_dec <= SAMPLE_ROWS
    assert cache_k.shape[3:] == (N_HEADS, HEAD_DIM)

    xp = x_prompt.reshape(bp * t_len, d_model)
    xs = jnp.pad(x_sample, ((0, 0), (0, SAMPLE_ROWS - t_dec), (0, 0))).reshape(bs * SAMPLE_ROWS, d_model)
    tm_p, tm_s = TM_PROMPT, bs * SAMPLE_ROWS

    tbl = rel_bias.astype(F32).T

    conv_prev_p = jnp.zeros((1, bp, SUBLANES, conv_ch), F32)
    conv_prev_s = jnp.pad(state_conv.astype(F32), ((0, 0), (0, 0), (SUBLANES - (CONV_W - 1), 0), (0, 0)))
    s0_p = jnp.zeros((1, bp, N_HEADS, HEAD_DIM, HEAD_DIM), F32)
    s0_s = state_delta.astype(F32)

    w_main, w_ab = _split_w_in(w_in, a_off, mb_off)
    w_out16, wg, wu, wd = w_out.astype(BF16), w_gate.astype(BF16), w_up.astype(BF16), w_down.astype(BF16)

    outs = {name: [] for name in ("kp", "vp", "cp", "sp", "kd", "vd", "cd", "sd")}
    for l in range(n_layers):
        nm, nf = norm_mix[l].reshape(1, d_model), norm_ffn[l].reshape(1, d_model)
        alog_row, dtb_row, gn_row = _pad_row(a_log[l]), _pad_row(dt_bias[l]), gn_w[l].reshape(1, HEAD_DIM)

        proj, gates = _inproj(xp, nm, w_main, w_ab, alog_row, dtb_row, l, tm_p, DN_CHUNK, DN_CHUNK)
        mix_dn, s_new = _deltanet(proj, gates, conv_prev_p, conv_w[l], gn_row, s0_p, l, bp, t_len, t_len)
        mix_mb = _moba_prompt(tbl, proj, bp, t_len)
        xp = _outproj(xp, mix_dn, mix_mb, w_out16, l, tm_p)
        act, kmean = _ffn_up(xp, nf, wg, wu, l, tm_p, cache_k, page_table)
        xp = _ffn_down(xp, act, wd, l, tm_p)
        kmean_t = kmean.transpose(0, 2, 1, 3)
        proj3 = proj.reshape(bp, t_len, MAIN_COLS)
        outs["kp"].append(proj3[:, :, KMB_BLK * HEAD_DIM:VMB_BLK * HEAD_DIM].reshape(bp, t_len, N_HEADS, HEAD_DIM))
        outs["vp"].append(proj3[:, :, VMB_BLK * HEAD_DIM:].reshape(bp, t_len, N_HEADS, HEAD_DIM))
        outs["cp"].append(proj3[:, t_len - (CONV_W - 1):, :conv_ch])
        outs["sp"].append(s_new)

        proj, gates = _inproj(xs, nm, w_main, w_ab, alog_row, dtb_row, l, tm_s, SAMPLE_ROWS, t_dec)
        mix_dn, s_new = _deltanet(proj, gates, conv_prev_s, conv_w[l], gn_row, s0_s, l, bs, SAMPLE_ROWS, t_dec)
        sel = _sample_select(proj, kmean_t, bs)
        sel_flat = sel[:, :, :t_dec, :MOBA_TOPK].reshape(-1)
        mix_mb = _moba_sample(page_table, sel_flat, tbl, proj, cache_k, cache_v, l, bs, t_dec)
        xs = _outproj(xs, mix_dn, mix_mb, w_out16, l, tm_s)
        xs = _ffn_down(xs, _ffn_up(xs, nf, wg, wu, l, tm_s), wd, l, tm_s)
        proj3 = proj.reshape(bs, SAMPLE_ROWS, MAIN_COLS)
        outs["kd"].append(proj3[:, :t_dec, KMB_BLK * HEAD_DIM:VMB_BLK * HEAD_DIM].reshape(bs, t_dec, N_HEADS, HEAD_DIM))
        outs["vd"].append(proj3[:, :t_dec, VMB_BLK * HEAD_DIM:].reshape(bs, t_dec, N_HEADS, HEAD_DIM))
        outs["cd"].append(proj3[:, t_dec - (CONV_W - 1):t_dec, :conv_ch])
        outs["sd"].append(s_new)

    fn = final_norm.reshape(1, d_model)
    y_prompt = _final_norm(xp, fn, tm_p).reshape(bp, t_len, d_model)
    y_sample = _final_norm(xs, fn, tm_s).reshape(bs, SAMPLE_ROWS, d_model)[:, :t_dec]
    return (y_prompt, y_sample) + tuple(jnp.stack(outs[n]) for n in ("kp", "vp", "cp", "sp", "kd", "vd", "cd", "sd"))
```

```python
import functools
import math

import jax
import jax.numpy as jnp
import numpy as np
from jax import lax
from jax.experimental import pallas as pl
from jax.experimental.pallas import tpu as pltpu

F32 = jnp.float32
BF16 = jnp.bfloat16
HI = lax.Precision.HIGHEST

HEAD_DIM = 128
N_HEADS = 8
GROUP_W = N_HEADS * HEAD_DIM
CONV_W = 4
DN_CHUNK = 128
DN_HEADS_PER_STEP = 4
DN_CHAINS = 8
MOBA_BLOCK = 256
MOBA_TOPK = 3
MOBA_HEADS_PER_STEP = 4
REL_BUCKETS = 32
REL_MAX_DIST = 128
RMS_EPS = 1e-6
L2_EPS = 1e-6
SUBLANES = 8
SAMPLE_ROWS = 8

QDN_BLK, KDN_BLK, VDN_BLK, Z_BLK, QMB_BLK, KMB_BLK, VMB_BLK = 0, 8, 16, 24, 32, 40, 48
MAIN_COLS = 56 * HEAD_DIM

TM_PROMPT = 1024
TN_INPROJ = 1024
TN_OUTPROJ = 1024
TN_FFN_UP = 512
TN_FFN_DOWN = 512
PAGES_PER_STEP = 16
TC_W_IN = 512
VMEM_LIMIT = 60 * 1024 * 1024


def _bucket_thresholds():
    d = np.arange(0, 4 * REL_MAX_DIST)
    max_exact = REL_BUCKETS // 2
    nf = np.maximum(d, 1).astype(np.float32)
    large = max_exact + (np.log(nf / np.float32(max_exact)) / np.float32(math.log(REL_MAX_DIST / max_exact))
                         * (REL_BUCKETS - max_exact)).astype(np.int32)
    bucket = np.where(d < max_exact, d, np.minimum(large, REL_BUCKETS - 1))
    return tuple(int(np.argmax(bucket >= b)) for b in range(REL_BUCKETS))


BUCKET_THR = _bucket_thresholds()


def _cparams(*sem):
    return pltpu.CompilerParams(dimension_semantics=sem, vmem_limit_bytes=VMEM_LIMIT)


def _rms_rows(x, w_row):
    return x * lax.rsqrt(jnp.mean(x * x, axis=-1, keepdims=True) + RMS_EPS) * w_row


def _dot(a, b, precision=None):
    return jnp.dot(a, b, preferred_element_type=F32, precision=precision)


def _dot_nt(a, b, precision=None):
    return lax.dot_general(a, b, (((1,), (1,)), ((), ())), preferred_element_type=F32, precision=precision)


def _split_bf16(a):
    hi = a.astype(BF16)
    return hi, (a - hi.astype(F32)).astype(BF16)


def _dot3(a_parts, b_parts):
    (ah, al), (bh, bl) = a_parts, b_parts
    return _dot(ah, bh) + (_dot(ah, bl) + _dot(al, bh))


def _split_w_in_kernel(w_ref, ab_ref, o_ref, oab_ref):
    o_ref[...] = w_ref[0].T.astype(BF16)

    @pl.when(pl.program_id(1) == 0)
    def _():
        ab = ab_ref[0].T
        pad = jnp.zeros((ab.shape[0], HEAD_DIM - ab.shape[1]), F32)
        oab_ref[...] = jnp.concatenate([ab, pad], axis=1).astype(BF16)


def _split_w_in(w_in, a_off, mb_off):
    n_layers, d, cols = w_in.shape
    tc = TC_W_IN
    assert a_off + (cols - mb_off) == MAIN_COLS and a_off % tc == 0 and MAIN_COLS % tc == 0
    assert a_off % SUBLANES == 0 and mb_off % SUBLANES == 0
    w_t = jnp.swapaxes(w_in, 1, 2)

    def rows(l, c):
        return (l, pl.multiple_of(c * tc + jnp.where(c * tc >= a_off, mb_off - a_off, 0), SUBLANES), 0)

    return pl.pallas_call(
        _split_w_in_kernel,
        grid=(n_layers, MAIN_COLS // tc),
        in_specs=[pl.BlockSpec((pl.Element(1), pl.Element(tc), pl.Element(d)), rows),
                  pl.BlockSpec((pl.Element(1), pl.Element(mb_off - a_off), pl.Element(d)), lambda l, c: (l, a_off, 0))],
        out_specs=[pl.BlockSpec((None, d, tc), lambda l, c: (l, 0, c)),
                   pl.BlockSpec((None, d, HEAD_DIM), lambda l, c: (l, 0, 0))],
        out_shape=[jax.ShapeDtypeStruct((n_layers, d, MAIN_COLS), BF16),
                   jax.ShapeDtypeStruct((n_layers, d, HEAD_DIM), BF16)],
        compiler_params=_cparams("arbitrary", "arbitrary"),
        name="split_w_in",
    )(w_t, w_t)


def _inproj_kernel(x_ref, nw_ref, w_ref, wab_ref, alog_ref, dtb_ref, o_ref, ogb_ref, h_scr, *, seg, t_valid):
    @pl.when(pl.program_id(1) == 0)
    def _():
        hb = _rms_rows(x_ref[...], nw_ref[...]).astype(BF16)
        h_scr[...] = hb
        ab = _dot(hb, wab_ref[...])
        tm = ab.shape[0]
        ch = min(tm, DN_CHUNK)
        z_ab = ab + dtb_ref[...]
        softplus = jnp.maximum(z_ab, 0.0) + jnp.log1p(jnp.exp(-jnp.abs(z_ab)))
        g = -jnp.exp(alog_ref[...]) * softplus
        beta = jax.nn.sigmoid(ab)
        if t_valid < seg:
            valid = (lax.broadcasted_iota(jnp.int32, ab.shape, 0) & (seg - 1)) < t_valid
            g, beta = jnp.where(valid, g, 0.0), jnp.where(valid, beta, 0.0)
        ii = lax.broadcasted_iota(jnp.int32, (ch, ch), 0)
        jj = lax.broadcasted_iota(jnp.int32, (ch, ch), 1)
        same_run_before = ((ii >= jj) & ((ii - jj) <= (ii & (seg - 1)))).astype(F32)
        is_g = lax.broadcasted_iota(jnp.int32, (ch, HEAD_DIM), 1) < N_HEADS
        for c0 in range(0, tm, ch):
            gc = _dot(same_run_before, g[c0:c0 + ch], HI)
            ogb_ref[c0:c0 + ch, :] = jnp.where(is_g, gc, beta[c0:c0 + ch])

    o_ref[...] = _dot(h_scr[...], w_ref[...])


def _inproj(x, nw, w_main, w_ab, alog_row, dtb_row, layer, tm, seg, t_valid):
    m, d = x.shape
    n = w_main.shape[2]
    tn = TN_INPROJ
    assert seg & (seg - 1) == 0 and min(tm, DN_CHUNK) % seg == 0 and tm % min(tm, DN_CHUNK) == 0
    row_spec = pl.BlockSpec((1, HEAD_DIM), lambda i, j: (0, 0))
    return pl.pallas_call(
        functools.partial(_inproj_kernel, seg=seg, t_valid=t_valid),
        grid=(m // tm, n // tn),
        in_specs=[pl.BlockSpec((tm, d), lambda i, j: (i, 0)),
                  pl.BlockSpec((1, d), lambda i, j: (0, 0)),
                  pl.BlockSpec((None, d, tn), lambda i, j: (layer, 0, j)),
                  pl.BlockSpec((None, d, HEAD_DIM), lambda i, j: (layer, 0, 0)),
                  row_spec, row_spec],
        out_specs=[pl.BlockSpec((tm, tn), lambda i, j: (i, j)),
                   pl.BlockSpec((tm, HEAD_DIM), lambda i, j: (i, 0))],
        out_shape=[jax.ShapeDtypeStruct((m, n), F32), jax.ShapeDtypeStruct((m, HEAD_DIM), F32)],
        scratch_shapes=[pltpu.VMEM((tm, d), BF16)],
        compiler_params=_cparams("parallel", "arbitrary"),
        name="inproj",
    )(x, nw, w_main, w_ab, alog_row, dtb_row)


def _outproj_kernel(x_ref, a_ref, b_ref, wa_ref, wb_ref, o_ref):
    o_ref[...] = x_ref[...] + (_dot(a_ref[...], wa_ref[...]) + _dot(b_ref[...], wb_ref[...]))


def _outproj(x, mix_dn, mix_mb, w_out, layer, tm):
    m, d = x.shape
    k = mix_dn.shape[1]
    tn = TN_OUTPROJ
    return pl.pallas_call(
        _outproj_kernel,
        grid=(m // tm, d // tn),
        in_specs=[pl.BlockSpec((tm, tn), lambda i, j: (i, j)),
                  pl.BlockSpec((tm, k), lambda i, j: (i, 0)),
                  pl.BlockSpec((tm, k), lambda i, j: (i, 0)),
                  pl.BlockSpec((None, k, tn), lambda i, j: (layer, 0, j)),
                  pl.BlockSpec((None, k, tn), lambda i, j: (layer, 1, j))],
        out_specs=pl.BlockSpec((tm, tn), lambda i, j: (i, j)),
        out_shape=jax.ShapeDtypeStruct((m, d), F32),
        compiler_params=_cparams("parallel", "arbitrary"),
        name="outproj",
    )(x, mix_dn, mix_mb, w_out, w_out)


def _ffn_up_kernel(*refs, n_mean_steps, n_col_steps):
    if n_mean_steps:
        refs = refs[1:]
    x_ref, nw_ref, wg_ref, wu_ref = refs[:4]
    if n_mean_steps:
        pages = refs[4:4 + PAGES_PER_STEP]
        o_ref, km_ref, h_scr = refs[4 + PAGES_PER_STEP:]
    else:
        o_ref, h_scr = refs[4:]

    @pl.when(pl.program_id(1) == 0)
    def _():
        h_scr[...] = _rms_rows(x_ref[...], nw_ref[...]).astype(BF16)

    h = h_scr[...]
    g = _dot(h, wg_ref[...])
    u = _dot(h, wu_ref[...])
    o_ref[...] = (g * jax.nn.sigmoid(g) * u).astype(BF16)

    if n_mean_steps:
        @pl.when(pl.program_id(0) * n_col_steps + pl.program_id(1) < n_mean_steps)
        def _():
            pages_per_block = PAGES_PER_STEP // SUBLANES
            for n in range(SUBLANES):
                tot = jnp.sum(pages[n * pages_per_block][...], axis=0)
                for p in range(1, pages_per_block):
                    tot = tot + jnp.sum(pages[n * pages_per_block + p][...], axis=0)
                km_ref[n] = tot * (1.0 / MOBA_BLOCK)


def _ffn_up(x, nw, wg, wu, layer, tm, cache_k=None, page_table=None):
    m, d = x.shape
    n = wg.shape[2]
    tn = TN_FFN_UP
    grid = (m // tm, n // tn)
    in_specs = [pl.BlockSpec((tm, d), lambda i, j, *_: (i, 0)),
                pl.BlockSpec((1, d), lambda i, j, *_: (0, 0)),
                pl.BlockSpec((None, d, tn), lambda i, j, *_: (layer, 0, j)),
                pl.BlockSpec((None, d, tn), lambda i, j, *_: (layer, 0, j))]
    out_specs = [pl.BlockSpec((tm, tn), lambda i, j, *_: (i, j))]
    out_shape = [jax.ShapeDtypeStruct((m, n), BF16)]
    operands = [x, nw, wg, wu]
    n_mean_steps = 0
    if cache_k is not None:
        _, _, page, n_heads, hd = cache_k.shape
        n_seq, n_pages = page_table.shape
        assert MOBA_BLOCK % page == 0 and PAGES_PER_STEP * page == SUBLANES * MOBA_BLOCK
        steps_per_seq = n_pages // PAGES_PER_STEP
        n_mean_steps = n_seq * steps_per_seq
        assert n_pages % PAGES_PER_STEP == 0 and n_mean_steps <= grid[0] * grid[1]

        def mean_step(i, j):
            ks = jnp.minimum(i * grid[1] + j, n_mean_steps - 1)
            return ks // steps_per_seq, ks % steps_per_seq

        def page_spec(p):
            def index(i, j, pt):
                b, nb = mean_step(i, j)
                return (layer, pt[b, nb * PAGES_PER_STEP + p], 0, 0, 0)
            return pl.BlockSpec((None, None, page, n_heads, hd), index)

        in_specs += [page_spec(p) for p in range(PAGES_PER_STEP)]
        out_specs.append(pl.BlockSpec((None, SUBLANES, n_heads, hd), lambda i, j, pt: (*mean_step(i, j), 0, 0)))
        out_shape.append(jax.ShapeDtypeStruct((n_seq, n_pages * page // MOBA_BLOCK, n_heads, hd), F32))
        operands = [page_table] + operands + [cache_k] * PAGES_PER_STEP
    res = pl.pallas_call(
        functools.partial(_ffn_up_kernel, n_mean_steps=n_mean_steps, n_col_steps=grid[1]),
        grid_spec=pltpu.PrefetchScalarGridSpec(
            num_scalar_prefetch=1 if n_mean_steps else 0, grid=grid, in_specs=in_specs, out_specs=out_specs,
            scratch_shapes=[pltpu.VMEM((tm, d), BF16)]),
        out_shape=out_shape,
        compiler_params=_cparams("arbitrary", "arbitrary"),
        name="ffn_up",
    )(*operands)
    return res if n_mean_steps else res[0]


def _ffn_down_kernel(x_ref, a_ref, w_ref, o_ref):
    o_ref[...] = x_ref[...] + _dot(a_ref[...], w_ref[...])


def _ffn_down(x, act, wd, layer, tm):
    m, d = x.shape
    k = act.shape[1]
    tn = TN_FFN_DOWN
    return pl.pallas_call(
        _ffn_down_kernel,
        grid=(m // tm, d // tn),
        in_specs=[pl.BlockSpec((tm, tn), lambda i, j: (i, j)),
                  pl.BlockSpec((tm, k), lambda i, j: (i, 0)),
                  pl.BlockSpec((None, k, tn), lambda i, j: (layer, 0, j))],
        out_specs=pl.BlockSpec((tm, tn), lambda i, j: (i, j)),
        out_shape=jax.ShapeDtypeStruct((m, d), F32),
        compiler_params=_cparams("parallel", "arbitrary"),
        name="ffn_down",
    )(x, act, wd)


def _final_norm_kernel(x_ref, nw_ref, o_ref):
    o_ref[...] = _rms_rows(x_ref[...], nw_ref[...])


def _final_norm(x, nw, tm):
    m, d = x.shape
    return pl.pallas_call(
        _final_norm_kernel,
        grid=(m // tm,),
        in_specs=[pl.BlockSpec((tm, d), lambda i: (i, 0)), pl.BlockSpec((1, d), lambda i: (0, 0))],
        out_specs=pl.BlockSpec((tm, d), lambda i: (i, 0)),
        out_shape=jax.ShapeDtypeStruct((m, d), F32),
        compiler_params=_cparams("parallel"),
        name="final_norm",
    )(x, nw)


def _deltanet_kernel(q_ref, k_ref, v_ref, z_ref, gb_ref, pq_ref, pk_ref, pv_ref, cq_ref, ck_ref, cv_ref,
                     gn_ref, s0_ref, o_ref, s_ref,
                     u_scr, wq_scr, kdt_scr, attn_scr, egl_scr, *, t_rows, t_valid):
    c = DN_CHUNK
    hd = HEAD_DIM
    heads = range(DN_HEADS_PER_STEP)
    head0 = pl.program_id(1) * DN_HEADS_PER_STEP
    n_chunks = max(t_rows // c, 1)
    rows = min(t_rows, c)
    n_doublings = max((min(c, t_valid) - 1).bit_length() - 1, 0)
    group = math.gcd(n_chunks, max(DN_CHAINS // DN_HEADS_PER_STEP, 1))

    ii = lax.broadcasted_iota(jnp.int32, (c, c), 0)
    jj = lax.broadcasted_iota(jnp.int32, (c, c), 1)
    tril = ii >= jj
    strict = ii > jj
    eye_f = (ii == jj).astype(F32)
    ones_cc = jnp.ones((c, c), F32)
    lane = lax.broadcasted_iota(jnp.int32, (rows, hd), 1)
    gn = gn_ref[...]

    def each(fn, *lists):
        return [fn(*args) for args in zip(*lists)]

    def cols(hh):
        return slice(hh * hd, (hh + 1) * hd)

    def load(ref, r0, hh):
        x = ref[pl.ds(r0, rows), cols(hh)]
        if rows < c:
            x = jnp.concatenate([x, jnp.zeros((c - rows, hd), F32)], axis=0)
        return x

    def conv_silu(ref, prev_ref, taps_ref, ci, hh):
        r0 = pl.multiple_of(ci * rows, rows)
        x = load(ref, r0, hh)
        tail = prev_ref[:, cols(hh)]
        if n_chunks > 1:
            before = ref[pl.ds(pl.multiple_of(jnp.maximum(r0 - SUBLANES, 0), SUBLANES), SUBLANES), cols(hh)]
            tail = jnp.where(ci == 0, tail, before)
        win = jnp.concatenate([tail, x], axis=0)
        taps = taps_ref[:, cols(hh)]
        y = win[5:5 + c] * taps[0:1]
        for i in range(1, CONV_W):
            y = y + win[5 + i:5 + i + c] * taps[i:i + 1]
        return y * jax.nn.sigmoid(y)

    def l2n(x):
        return x * lax.rsqrt(jnp.sum(x * x, axis=-1, keepdims=True) + L2_EPS)

    def gates(ci, hh):
        gb = gb_ref[pl.ds(pl.multiple_of(ci * rows, rows), rows), :]
        gc = jnp.sum(jnp.where(lane == head0 + hh, gb, 0.0), axis=1, keepdims=True)
        beta = jnp.sum(jnp.where(lane == N_HEADS + head0 + hh, gb, 0.0), axis=1, keepdims=True)
        if rows < c:
            gc = jnp.concatenate([gc, jnp.broadcast_to(gc[rows - 1:rows], (c - rows, 1))], axis=0)
            beta = jnp.concatenate([beta, jnp.zeros((c - rows, 1), F32)], axis=0)
        return gc, beta

    def prepare(gi, carry):
        cis = [gi * group + t for t in range(group) for _ in heads]
        hhs = [hh for _ in range(group) for hh in heads]
        q = each(lambda ci, hh: l2n(conv_silu(q_ref, pq_ref, cq_ref, ci, hh)) * (hd ** -0.5), cis, hhs)
        k = each(lambda ci, hh: l2n(conv_silu(k_ref, pk_ref, ck_ref, ci, hh)), cis, hhs)
        v = each(lambda ci, hh: conv_silu(v_ref, pv_ref, cv_ref, ci, hh), cis, hhs)
        gc, beta = zip(*each(gates, cis, hhs))
        gc_row = each(lambda g: _dot(ones_cc, eye_f * g, HI), gc)
        decay = each(lambda g, gr: jnp.where(tril, jnp.exp(jnp.where(tril, g - gr, 0.0)), 0.0), gc, gc_row)
        e_gc = each(jnp.exp, gc)
        g_last = each(lambda g: g[c - 1:c, :], gc)
        kb = each(lambda a, bt: a * bt, k, beta)
        vb = each(lambda a, bt: a * bt, v, beta)
        k16 = each(lambda a: a.astype(BF16), k)
        neg_l = each(lambda a, b16, dc: jnp.where(strict, -(_dot_nt(a.astype(BF16), b16) * dc), 0.0), kb, k16, decay)
        tinv = each(lambda m: eye_f + m, neg_l)
        p = each(_split_bf16, neg_l)
        for _ in range(n_doublings):
            p = each(lambda pp: _split_bf16(_dot3(pp, pp)), p)
            tinv = each(lambda t, pp: t + _dot3(_split_bf16(t), pp), tinv, p)
        uw = each(lambda t, a, b, e: _dot3(_split_bf16(t), _split_bf16(jnp.concatenate([a, b * e], axis=1))),
                  tinv, vb, kb, e_gc)
        attn = each(lambda a, b16, dc: jnp.where(tril, _dot_nt(a.astype(BF16), b16) * dc, 0.0).astype(BF16),
                    q, k16, decay)
        kdt = each(lambda a, gl, g: (a * jnp.exp(gl - g)).T.astype(BF16), k, g_last, gc)
        for n, (ci, hh) in enumerate(zip(cis, hhs)):
            c0 = pl.multiple_of(ci * c, c)
            u_scr[hh, pl.ds(c0, c), :] = uw[n][:, :hd]
            wq_scr[hh, pl.ds(2 * c0, c), :] = uw[n][:, hd:].astype(BF16)
            wq_scr[hh, pl.ds(2 * c0 + c, c), :] = (q[n] * e_gc[n]).astype(BF16)
            attn_scr[hh, pl.ds(c0, c), :] = attn[n]
            kdt_scr[hh, pl.ds(c0, c), :] = kdt[n]
            egl_scr[hh, pl.ds(pl.multiple_of(ci * SUBLANES, SUBLANES), SUBLANES), :] = jnp.broadcast_to(
                jnp.exp(g_last[n]), (SUBLANES, hd))
        return carry

    lax.fori_loop(0, n_chunks // group, prepare, 0)

    def recur(ci, states):
        r0 = pl.multiple_of(ci * rows, rows)
        c0 = pl.multiple_of(ci * c, c)
        e0 = pl.multiple_of(ci * SUBLANES, SUBLANES)
        r = each(lambda hh, s: _dot(wq_scr[hh, pl.ds(2 * c0, 2 * c), :], s.astype(BF16)), heads, states)
        v_new = each(lambda hh, rr: (u_scr[hh, pl.ds(c0, c), :] - rr[:c]).astype(BF16), heads, r)
        o = each(lambda hh, rr, vn: rr[c:] + _dot(attn_scr[hh, pl.ds(c0, c), :], vn), heads, r, v_new)
        new_states = each(lambda hh, s, vn: s * egl_scr[hh, pl.ds(e0, SUBLANES), :][0:1]
                          + _dot(kdt_scr[hh, pl.ds(c0, c), :], vn), heads, states, v_new)
        for hh in heads:
            on = o[hh] * lax.rsqrt(jnp.mean(o[hh] * o[hh], axis=-1, keepdims=True) + RMS_EPS) * gn
            zc = load(z_ref, r0, hh)
            out = (on * (zc * jax.nn.sigmoid(zc))).astype(o_ref.dtype)
            o_ref[pl.ds(r0, rows), cols(hh)] = out[:rows]
        return tuple(new_states)

    final = lax.fori_loop(0, n_chunks, recur, tuple(s0_ref[hh] for hh in heads))
    for hh in heads:
        s_ref[hh] = final[hh]


def _deltanet(proj, gates, conv_prev, conv_w, gn_row, s0, layer, n_seq, t_rows, t_valid):
    lp = layer if conv_prev.shape[0] > 1 else 0
    t_pad = max(t_rows // DN_CHUNK, 1) * DN_CHUNK
    n_chunks = t_pad // DN_CHUNK
    hps = DN_HEADS_PER_STEP
    width = hps * HEAD_DIM
    assert N_HEADS % hps == 0

    def col(off):
        return pl.BlockSpec((t_rows, width), lambda b, h: (b, off // hps + h))

    def prev(off):
        return pl.BlockSpec((None, None, SUBLANES, width), lambda b, h: (lp, b, 0, off // hps + h))

    def taps(off):
        return pl.BlockSpec((CONV_W, width), lambda b, h: (0, off // hps + h))

    row_spec = pl.BlockSpec((1, HEAD_DIM), lambda b, h: (0, 0))
    return pl.pallas_call(
        functools.partial(_deltanet_kernel, t_rows=t_rows, t_valid=t_valid),
        grid=(n_seq, N_HEADS // hps),
        in_specs=[col(QDN_BLK), col(KDN_BLK), col(VDN_BLK), col(Z_BLK),
                  pl.BlockSpec((t_rows, HEAD_DIM), lambda b, h: (b, 0)),
                  prev(QDN_BLK), prev(KDN_BLK), prev(VDN_BLK),
                  taps(QDN_BLK), taps(KDN_BLK), taps(VDN_BLK),
                  row_spec,
                  pl.BlockSpec((None, None, hps, HEAD_DIM, HEAD_DIM), lambda b, h: (lp, b, h, 0, 0))],
        out_specs=[pl.BlockSpec((t_rows, width), lambda b, h: (b, h)),
                   pl.BlockSpec((None, hps, HEAD_DIM, HEAD_DIM), lambda b, h: (b, h, 0, 0))],
        out_shape=[jax.ShapeDtypeStruct((n_seq * t_rows, GROUP_W), BF16),
                   jax.ShapeDtypeStruct((n_seq, N_HEADS, HEAD_DIM, HEAD_DIM), F32)],
        scratch_shapes=[pltpu.VMEM((hps, t_pad, HEAD_DIM), F32),
                        pltpu.VMEM((hps, 2 * t_pad, HEAD_DIM), BF16),
                        pltpu.VMEM((hps, t_pad, DN_CHUNK), BF16),
                        pltpu.VMEM((hps, t_pad, DN_CHUNK), BF16),
                        pltpu.VMEM((hps, n_chunks * SUBLANES, HEAD_DIM), F32)],
        compiler_params=_cparams("parallel", "arbitrary"),
        name="deltanet",
    )(proj, proj, proj, proj, gates, conv_prev, conv_prev, conv_prev, conv_w, conv_w, conv_w, gn_row, s0)


def _rel_bias_tile(tbl_ref, h, dist):
    val = jnp.full(dist.shape, tbl_ref[h, 0], F32)
    for b in range(1, REL_BUCKETS):
        val = jnp.where(dist >= BUCKET_THR[b], tbl_ref[h, b], val)
    return val


def _gate_rank(gate, n_cand, idx, axis):
    rank = jnp.zeros(gate.shape, F32)
    for m in range(n_cand):
        gm = lax.slice_in_dim(gate, m, m + 1, axis=axis)
        ahead = (gm > gate) | ((gm == gate) & (m < idx))
        rank = rank + ahead.astype(F32)
    return rank


def _moba_prompt_kernel(tbl_ref, q_ref, k_ref, v_ref, o_ref, kmean_scr, bown_scr, bnear_scr, k16_scr, vt16_scr,
                        *, n_blocks):
    blk = MOBA_BLOCK
    hd = HEAD_DIM
    heads = range(MOBA_HEADS_PER_STEP)
    head0 = pl.program_id(0) * MOBA_HEADS_PER_STEP
    b = pl.program_id(1)
    j = pl.program_id(2)
    scale = hd ** -0.5

    def each(fn, *lists):
        return [fn(*args) for args in zip(*lists)]

    def cols(hh):
        return slice(hh * hd, (hh + 1) * hd)

    @pl.when((b == 0) & (j == 0))
    def _():
        ki = lax.broadcasted_iota(jnp.int32, (blk, blk), 0)
        qi = lax.broadcasted_iota(jnp.int32, (blk, blk), 1)
        d = qi - ki
        for hh in heads:
            bown_scr[hh] = jnp.where(d >= 0, _rel_bias_tile(tbl_ref, head0 + hh, d), -jnp.inf)
            bnear_scr[hh] = _rel_bias_tile(tbl_ref, head0 + hh, d + blk)

    @pl.when(j == 0)
    def _():
        for hh in heads:
            kmean_scr[hh] = jnp.zeros(kmean_scr.shape[1:], F32)
            for n in range(n_blocks):
                kmean_scr[hh, n:n + 1, :] = jnp.sum(k_ref[n * blk:(n + 1) * blk, cols(hh)], axis=0,
                                                    keepdims=True) * (1.0 / blk)
            k16_scr[hh] = k_ref[:, cols(hh)].astype(BF16)
            for n in range(n_blocks * blk // hd):
                rows = slice(n * hd, (n + 1) * hd)
                vt16_scr[hh, :, rows] = v_ref[rows, cols(hh)].T.astype(BF16)

    far_bias = [tbl_ref[head0 + hh, REL_BUCKETS - 1] for hh in heads]

    def attend(jb):
        q = [q_ref[:, cols(hh)] for hh in heads]
        q16 = each(lambda a: a.astype(BF16), q)
        n_keys = (jb + 1) * blk
        gated = jb > MOBA_TOPK
        if gated:
            cand = lax.broadcasted_iota(jnp.int32, (SUBLANES, blk), 0)
            gate_t = each(lambda hh, a: jnp.where(cand < jb, _dot_nt(kmean_scr[hh], a, HI)[:SUBLANES], -jnp.inf),
                          heads, q)
            keep_t = each(lambda g: (_gate_rank(g, jb, cand, 0) < MOBA_TOPK).astype(F32), gate_t)
        pieces = [[] for _ in heads]
        for n in range(jb + 1):
            for hh in heads:
                sn = _dot_nt(k16_scr[hh, n * blk:(n + 1) * blk, :], q16[hh]) * scale
                if n == jb:
                    sn = sn + bown_scr[hh]
                else:
                    sn = sn + (bnear_scr[hh] if n == jb - 1 else far_bias[hh])
                    if gated:
                        sn = jnp.where(keep_t[hh][n:n + 1, :] > 0.0, sn, -jnp.inf)
                pieces[hh].append(sn)

        def col_max(ps):
            m = jnp.max(ps[0], axis=0, keepdims=True)
            for sn in ps[1:]:
                m = jnp.maximum(m, jnp.max(sn, axis=0, keepdims=True))
            return m

        def col_sum(ps):
            tot = jnp.sum(ps[0], axis=0, keepdims=True)
            for pn in ps[1:]:
                tot = tot + jnp.sum(pn, axis=0, keepdims=True)
            return tot

        m = each(col_max, pieces)
        probs = each(lambda ps, mm: [jnp.exp(sn - mm) for sn in ps], pieces, m)
        denom = each(col_sum, probs)
        p = each(lambda ps: jnp.concatenate([pn.astype(BF16) for pn in ps], axis=0) if jb else ps[0].astype(BF16),
                 probs)
        o_t = each(lambda hh, pp, dn: _dot(vt16_scr[hh, :, 0:n_keys], pp) / dn, heads, p, denom)
        for hh in heads:
            o_ref[:, cols(hh)] = o_t[hh].T.astype(o_ref.dtype)

    for jb in range(n_blocks):
        pl.when(j == jb)(functools.partial(attend, jb))


def _moba_prompt(tbl, proj, n_seq, t_len):
    blk = MOBA_BLOCK
    n_blocks = t_len // blk
    hps = MOBA_HEADS_PER_STEP
    width = hps * HEAD_DIM
    assert MOBA_BLOCK >= BUCKET_THR[-1] and n_blocks <= SUBLANES and N_HEADS % hps == 0
    return pl.pallas_call(
        functools.partial(_moba_prompt_kernel, n_blocks=n_blocks),
        grid=(N_HEADS // hps, n_seq, n_blocks),
        in_specs=[pl.BlockSpec(memory_space=pltpu.SMEM),
                  pl.BlockSpec((blk, width), lambda h, b, j: (b * n_blocks + j, QMB_BLK // hps + h)),
                  pl.BlockSpec((t_len, width), lambda h, b, j: (b, KMB_BLK // hps + h)),
                  pl.BlockSpec((t_len, width), lambda h, b, j: (b, VMB_BLK // hps + h))],
        out_specs=pl.BlockSpec((blk, width), lambda h, b, j: (b * n_blocks + j, h)),
        out_shape=jax.ShapeDtypeStruct((n_seq * t_len, GROUP_W), BF16),
        scratch_shapes=[pltpu.VMEM((hps, HEAD_DIM, HEAD_DIM), F32),
                        pltpu.VMEM((hps, blk, blk), F32), pltpu.VMEM((hps, blk, blk), F32),
                        pltpu.VMEM((hps, t_len, HEAD_DIM), BF16), pltpu.VMEM((hps, HEAD_DIM, t_len), BF16)],
        compiler_params=_cparams("arbitrary", "arbitrary", "arbitrary"),
        name="moba_prompt",
    )(tbl, proj, proj, proj)


def _sample_select_kernel(q_ref, km_ref, o_ref, *, n_past_blocks):
    lane = lax.broadcasted_iota(jnp.int32, (SAMPLE_ROWS, HEAD_DIM), 1)
    lane_f = lane.astype(F32)
    pad = jnp.zeros((HEAD_DIM - n_past_blocks, HEAD_DIM), F32)
    gates = [jnp.where(lane < n_past_blocks,
                       _dot_nt(q_ref[:, hh * HEAD_DIM:(hh + 1) * HEAD_DIM],
                               jnp.concatenate([km_ref[hh], pad], axis=0), HI), -jnp.inf)
             for hh in range(N_HEADS)]
    for hh, gate in enumerate(gates):
        rank = _gate_rank(gate, n_past_blocks, lane, 1)
        out = jnp.zeros((SAMPLE_ROWS, HEAD_DIM), F32)
        for r in range(MOBA_TOPK):
            idx = jnp.sum(jnp.where(rank == r, lane_f, 0.0), axis=1, keepdims=True)
            out = jnp.where(lane == r, idx, out)
        o_ref[hh] = out.astype(jnp.int32)


def _sample_select(proj, kmean_t, n_seq):
    n_past_blocks = kmean_t.shape[2]
    assert MOBA_TOPK <= n_past_blocks <= HEAD_DIM
    return pl.pallas_call(
        functools.partial(_sample_select_kernel, n_past_blocks=n_past_blocks),
        grid=(n_seq,),
        in_specs=[pl.BlockSpec((SAMPLE_ROWS, GROUP_W), lambda b: (b, QMB_BLK // N_HEADS)),
                  pl.BlockSpec((None, N_HEADS, n_past_blocks, HEAD_DIM), lambda b: (b, 0, 0, 0))],
        out_specs=pl.BlockSpec((None, N_HEADS, SAMPLE_ROWS, HEAD_DIM), lambda b: (b, 0, 0, 0)),
        out_shape=jax.ShapeDtypeStruct((n_seq, N_HEADS, SAMPLE_ROWS, HEAD_DIM), jnp.int32),
        compiler_params=_cparams("arbitrary"),
        name="sample_select",
    )(proj, kmean_t)


def _moba_sample_kernel(pt_ref, sel_ref, tbl_ref, q_ref, kn_ref, vn_ref, ck_hbm, cv_hbm, o_ref, kbuf, vbuf, sem,
                        *, layer, n_valid, past_len, page):
    blk = MOBA_BLOCK
    pages_per_block = blk // page
    b, h = pl.program_id(0), pl.program_id(1)
    step = b * N_HEADS + h
    n_steps = pl.num_programs(0) * N_HEADS
    slot = step % 2
    scale = HEAD_DIM ** -0.5

    def page_copies(bb, hh, sl):
        copies = []
        for i in range(n_valid):
            for s in range(MOBA_TOPK):
                blk_idx = sel_ref[((bb * N_HEADS + hh) * n_valid + i) * MOBA_TOPK + s]
                for p in range(pages_per_block):
                    pg = pt_ref[bb, blk_idx * pages_per_block + p]
                    dst_rows = pl.ds(((i * MOBA_TOPK + s) * pages_per_block + p) * page, page)
                    copies.append(pltpu.make_async_copy(ck_hbm.at[layer, pg, :, hh, :],
                                                        kbuf.at[sl, dst_rows, :], sem.at[0, sl]))
                    copies.append(pltpu.make_async_copy(cv_hbm.at[layer, pg, :, hh, :],
                                                        vbuf.at[sl, dst_rows, :], sem.at[1, sl]))
        return copies

    @pl.when(step == 0)
    def _():
        for cp in page_copies(b, h, slot):
            cp.start()

    @pl.when(step + 1 < n_steps)
    def _():
        nxt = step + 1
        for cp in page_copies(nxt // N_HEADS, nxt % N_HEADS, 1 - slot):
            cp.start()

    for cp in page_copies(b, h, slot):
        cp.wait()

    qb = q_ref[...].astype(BF16)
    qrow = lax.broadcasted_iota(jnp.int32, (SAMPLE_ROWS, blk), 0)
    koff = lax.broadcasted_iota(jnp.int32, (SAMPLE_ROWS, blk), 1)
    orow = lax.broadcasted_iota(jnp.int32, (SAMPLE_ROWS, HEAD_DIM), 0)

    pad = jnp.zeros((HEAD_DIM - SAMPLE_ROWS, HEAD_DIM), F32)
    kn = jnp.concatenate([kn_ref[...], pad], axis=0).astype(BF16)
    vn = jnp.concatenate([vn_ref[...], pad], axis=0).astype(BF16)
    d_own = qrow[:, :HEAD_DIM] - koff[:, :HEAD_DIM]
    l_own = _dot_nt(qb, kn) * scale + _rel_bias_tile(tbl_ref, h, d_own)
    l_own = jnp.where(d_own >= 0, l_own, -jnp.inf)
    m_own = jnp.max(l_own, axis=1, keepdims=True)

    queries = range(n_valid)
    blocks = range(MOBA_TOPK)

    def score(i, s):
        sel = sel_ref[((b * N_HEADS + h) * n_valid + i) * MOBA_TOPK + s]
        dist = (past_len + qrow) - (sel * blk + koff)
        keys = kbuf[slot, pl.ds((i * MOBA_TOPK + s) * blk, blk), :].astype(BF16)
        return _dot_nt(qb, keys) * scale + _rel_bias_tile(tbl_ref, h, dist)

    logits = [[score(i, s) for s in blocks] for i in queries]
    m = []
    for i in queries:
        mi = m_own
        for l_s in logits[i]:
            mi = jnp.maximum(mi, jnp.max(l_s, axis=1, keepdims=True))
        m.append(mi)
    p_own = [jnp.exp(l_own - m[i]) for i in queries]
    p_sel = [[jnp.exp(logits[i][s] - m[i]) for s in blocks] for i in queries]
    result = jnp.zeros((SAMPLE_ROWS, HEAD_DIM), F32)
    for i in queries:
        denom = jnp.sum(p_own[i], axis=1, keepdims=True)
        acc = _dot(p_own[i].astype(BF16), vn)
        for s in blocks:
            denom = denom + jnp.sum(p_sel[i][s], axis=1, keepdims=True)
            values = vbuf[slot, pl.ds((i * MOBA_TOPK + s) * blk, blk), :].astype(BF16)
            acc = acc + _dot(p_sel[i][s].astype(BF16), values)
        result = jnp.where(orow == i, acc / denom, result)
    o_ref[...] = result.astype(o_ref.dtype)


def _moba_sample(page_table, sel_flat, tbl, proj, cache_k, cache_v, layer, n_seq, n_valid):
    page = cache_k.shape[2]
    past_len = page_table.shape[1] * page
    assert past_len % MOBA_BLOCK == 0 and MOBA_BLOCK % page == 0
    buf_rows = n_valid * MOBA_TOPK * MOBA_BLOCK

    def new_rows(off):
        return pl.BlockSpec((SAMPLE_ROWS, HEAD_DIM), lambda b, h, pt, sel: (b, off + h))

    return pl.pallas_call(
        functools.partial(_moba_sample_kernel, layer=layer, n_valid=n_valid, past_len=past_len, page=page),
        grid_spec=pltpu.PrefetchScalarGridSpec(
            num_scalar_prefetch=2,
            grid=(n_seq, N_HEADS),
            in_specs=[pl.BlockSpec(memory_space=pltpu.SMEM),
                      new_rows(QMB_BLK), new_rows(KMB_BLK), new_rows(VMB_BLK),
                      pl.BlockSpec(memory_space=pl.ANY), pl.BlockSpec(memory_space=pl.ANY)],
            out_specs=pl.BlockSpec((SAMPLE_ROWS, HEAD_DIM), lambda b, h, pt, sel: (b, h)),
            scratch_shapes=[pltpu.VMEM((2, buf_rows, HEAD_DIM), F32), pltpu.VMEM((2, buf_rows, HEAD_DIM), F32),
                            pltpu.SemaphoreType.DMA((2, 2))],
        ),
        out_shape=jax.ShapeDtypeStruct((n_seq * SAMPLE_ROWS, GROUP_W), BF16),
        compiler_params=_cparams("arbitrary", "arbitrary"),
        name="moba_sample",
    )(page_table, sel_flat, tbl, proj, proj, proj, cache_k, cache_v)


def _pad_row(v):
    return jnp.pad(v.astype(F32), (0, HEAD_DIM - v.shape[0])).reshape(1, HEAD_DIM)


def kernel(x_prompt, x_sample, cache_k, cache_v, page_table, state_conv, state_delta, w_in, conv_w, a_log, dt_bias,
           gn_w, w_out, norm_mix, norm_ffn, w_gate, w_up, w_down, rel_bias, final_norm):
    n_layers = w_in.shape[0]
    bp, t_len, d_model = x_prompt.shape
    bs, t_dec, _ = x_sample.shape
    conv_ch = 3 * GROUP_W
    a_off = conv_ch + GROUP_W
    mb_off = a_off + 2 * N_HEADS
    assert t_len % MOBA_BLOCK == 0 and t_len % DN_CHUNK == 0 and CONV_W - 1 <= t_dec <= SAMPLE_ROWS
    assert cache_k.shape[3:] == (N_HEADS, HEAD_DIM)

    xp = x_prompt.reshape(bp * t_len, d_model)
    xs = jnp.pad(x_sample, ((0, 0), (0, SAMPLE_ROWS - t_dec), (0, 0))).reshape(bs * SAMPLE_ROWS, d_model)
    tm_p, tm_s = TM_PROMPT, bs * SAMPLE_ROWS

    tbl = rel_bias.astype(F32).T

    conv_prev_p = jnp.zeros((1, bp, SUBLANES, conv_ch), F32)
    conv_prev_s = jnp.pad(state_conv.astype(F32), ((0, 0), (0, 0), (SUBLANES - (CONV_W - 1), 0), (0, 0)))
    s0_p = jnp.zeros((1, bp, N_HEADS, HEAD_DIM, HEAD_DIM), F32)
    s0_s = state_delta.astype(F32)

    w_main, w_ab = _split_w_in(w_in, a_off, mb_off)
    w_out16, wg, wu, wd = w_out.astype(BF16), w_gate.astype(BF16), w_up.astype(BF16), w_down.astype(BF16)

    outs = {name: [] for name in ("kp", "vp", "cp", "sp", "kd", "vd", "cd", "sd")}
    for l in range(n_layers):
        nm, nf = norm_mix[l].reshape(1, d_model), norm_ffn[l].reshape(1, d_model)
        alog_row, dtb_row, gn_row = _pad_row(a_log[l]), _pad_row(dt_bias[l]), gn_w[l].reshape(1, HEAD_DIM)

        proj, gates = _inproj(xp, nm, w_main, w_ab, alog_row, dtb_row, l, tm_p, DN_CHUNK, DN_CHUNK)
        mix_dn, s_new = _deltanet(proj, gates, conv_prev_p, conv_w[l], gn_row, s0_p, l, bp, t_len, t_len)
        mix_mb = _moba_prompt(tbl, proj, bp, t_len)
        xp = _outproj(xp, mix_dn, mix_mb, w_out16, l, tm_p)
        act, kmean = _ffn_up(xp, nf, wg, wu, l, tm_p, cache_k, page_table)
        xp = _ffn_down(xp, act, wd, l, tm_p)
        kmean_t = kmean.transpose(0, 2, 1, 3)
        proj3 = proj.reshape(bp, t_len, MAIN_COLS)
        outs["kp"].append(proj3[:, :, KMB_BLK * HEAD_DIM:VMB_BLK * HEAD_DIM].reshape(bp, t_len, N_HEADS, HEAD_DIM))
        outs["vp"].append(proj3[:, :, VMB_BLK * HEAD_DIM:].reshape(bp, t_len, N_HEADS, HEAD_DIM))
        outs["cp"].append(proj3[:, t_len - (CONV_W - 1):, :conv_ch])
        outs["sp"].append(s_new)

        proj, gates = _inproj(xs, nm, w_main, w_ab, alog_row, dtb_row, l, tm_s, SAMPLE_ROWS, t_dec)
        mix_dn, s_new = _deltanet(proj, gates, conv_prev_s, conv_w[l], gn_row, s0_s, l, bs, SAMPLE_ROWS, t_dec)
        sel = _sample_select(proj, kmean_t, bs)
        sel_flat = sel[:, :, :t_dec, :MOBA_TOPK].reshape(-1)
        mix_mb = _moba_sample(page_table, sel_flat, tbl, proj, cache_k, cache_v, l, bs, t_dec)
        xs = _outproj(xs, mix_dn, mix_mb, w_out16, l, tm_s)
        xs = _ffn_down(xs, _ffn_up(xs, nf, wg, wu, l, tm_s), wd, l, tm_s)
        proj3 = proj.reshape(bs, SAMPLE_ROWS, MAIN_COLS)
        outs["kd"].append(proj3[:, :t_dec, KMB_BLK * HEAD_DIM:VMB_BLK * HEAD_DIM].reshape(bs, t_dec, N_HEADS, HEAD_DIM))
        outs["vd"].append(proj3[:, :t_dec, VMB_BLK * HEAD_DIM:].reshape(bs, t_dec, N_HEADS, HEAD_DIM))
        outs["cd"].append(proj3[:, t_dec - (CONV_W - 1):t_dec, :conv_ch])
        outs["sd"].append(s_new)

    fn = final_norm.reshape(1, d_model)
    y_prompt = _final_norm(xp, fn, tm_p).reshape(bp, t_len, d_model)
    y_sample = _final_norm(xs, fn, tm_s).reshape(bs, SAMPLE_ROWS, d_model)[:, :t_dec]
    return (y_prompt, y_sample) + tuple(jnp.stack(outs[n]) for n in ("kp", "vp", "cp", "sp", "kd", "vd", "cd", "sd"))
```
